```python
import jax
import jax.numpy as jnp
from jax import lax
import numpy as np

D_MODEL = 4096
BATCH = 1
SEQ = 8192
DEPTH = 4

GRID_W = 64
CTX_LEN = 256
HEAD_DIM = 128
N_HEADS = D_MODEL // HEAD_DIM
N_KV_HEADS = N_HEADS // 4
WINDOW = 128
BLOCK = 128
ROPE_THETA = 10000.0
POOL_WINDOWS = (2, 4, 8, 16)
N_POOL_GROUPS = len(POOL_WINDOWS)
POOL_GROUP = D_MODEL // N_POOL_GROUPS
CONV_WIDTH = 3
D_FF = 4 * D_MODEL
ADALN_RANK = D_MODEL // 4
N_MIXERS = 3
N_ATTN = (DEPTH + 2) // 3
N_POOL = (DEPTH + 1) // 3
N_CONV = DEPTH // 3
DN_ALPHA = (2 * DEPTH) ** 0.25
DN_BETA = (8 * DEPTH) ** -0.25
LN_EPS = 1e-5

kernel_name = 'hybrid_interleaved_dit_ctx_prefix'


def layer_norm(x, g, b):
    xf = x.astype(jnp.float32)
    mu = jnp.mean(xf, axis=-1, keepdims=True)
    var = jnp.mean(jnp.square(xf - mu), axis=-1, keepdims=True)
    y = (xf - mu) * lax.rsqrt(var + LN_EPS) * g.astype(jnp.float32) + b.astype(jnp.float32)
    return y.astype(x.dtype)


def adaln(cond, w_down, w_up, bias):
    m = (jax.nn.silu(cond) @ w_down) @ w_up + bias
    return jnp.split(m, 6, axis=-1)


def axial_rope_tables(n_tokens):
    rows = n_tokens // GRID_W
    row_pos = jnp.repeat(jnp.arange(rows, dtype=jnp.float32), GRID_W)
    col_pos = jnp.tile(jnp.arange(GRID_W, dtype=jnp.float32), rows)
    n_freq = HEAD_DIM // 4
    freqs = ROPE_THETA ** (-jnp.arange(n_freq, dtype=jnp.float32) / n_freq)
    ang = jnp.stack([row_pos[:, None] * freqs, col_pos[:, None] * freqs], axis=1)
    return jnp.cos(ang), jnp.sin(ang)


def apply_axial_rope(x, cos, sin):
    b, s, h, d = x.shape
    xr = x.astype(jnp.float32).reshape(b, s, h, 2, 2, d // 4)
    x1, x2 = xr[..., 0, :], xr[..., 1, :]
    cb, sb = cos[None, :, None], sin[None, :, None]
    out = jnp.stack([x1 * cb - x2 * sb, x1 * sb + x2 * cb], axis=-2)
    return out.reshape(b, s, h, d).astype(x.dtype)


def windowed_gqa_attention(h, hc, w_qkv, w_o, sink, cos, sin, with_ctx_out):
    b, s, d = h.shape
    nb = s // BLOCK
    grp = N_HEADS // N_KV_HEADS
    scale = HEAD_DIM ** -0.5
    splits = [N_HEADS * HEAD_DIM, (N_HEADS + N_KV_HEADS) * HEAD_DIM]

    def project(u):
        n = u.shape[1]
        q, k, v = jnp.split(u @ w_qkv, splits, axis=-1)
        return (q.reshape(b, n, N_HEADS, HEAD_DIM),
                k.reshape(b, n, N_KV_HEADS, HEAD_DIM),
                v.reshape(b, n, N_KV_HEADS, HEAD_DIM))

    q, k, v = project(h)
    q = apply_axial_rope(q, cos, sin).reshape(b, s, N_KV_HEADS, grp, HEAD_DIM)
    k = apply_axial_rope(k, cos, sin)
    qc, kc, vc = project(hc)
    qc = qc.reshape(b, -1, N_KV_HEADS, grp, HEAD_DIM)
    sink_logit = sink.astype(jnp.float32).reshape(N_KV_HEADS, grp)[None, :, :, None, None]

    def softmax_with_sink(logits):
        sl = jnp.broadcast_to(sink_logit, logits.shape[:-1] + (1,))
        p = jax.nn.softmax(jnp.concatenate([logits, sl], axis=-1), axis=-1)
        return p[..., :-1]

    n_band = 3 * BLOCK
    kpad = jnp.pad(k, ((0, 0), (BLOCK, BLOCK), (0, 0), (0, 0)))
    vpad = jnp.pad(v, ((0, 0), (BLOCK, BLOCK), (0, 0), (0, 0)))
    q_blocks = jnp.moveaxis(q.reshape(b, nb, BLOCK, N_KV_HEADS, grp, HEAD_DIM), 1, 0)
    kj = jnp.arange(n_band)
    qi = jnp.arange(BLOCK)
    in_window = jnp.abs(kj[None, :] - BLOCK - qi[:, None]) <= WINDOW

    def one_block(args):
        n, qb = args
        kb = lax.dynamic_slice_in_dim(kpad, n * BLOCK, n_band, axis=1)
        vb = lax.dynamic_slice_in_dim(vpad, n * BLOCK, n_band, axis=1)
        key_pos = (n - 1) * BLOCK + kj
        valid = in_window & ((key_pos >= 0) & (key_pos < s))[None, :]
        s_band = jnp.einsum('bqkgd,bjkd->bkgqj', qb, kb).astype(jnp.float32) * scale
        s_band = jnp.where(valid, s_band, -jnp.inf)
        s_ctx = jnp.einsum('bqkgd,bckd->bkgqc', qb, kc).astype(jnp.float32) * scale
        p = softmax_with_sink(jnp.concatenate([s_band, s_ctx], axis=-1)).astype(vb.dtype)
        return (jnp.einsum('bkgqj,bjkd->bqkgd', p[..., :n_band], vb)
                + jnp.einsum('bkgqc,bckd->bqkgd', p[..., n_band:], vc))

    o = lax.map(one_block, (jnp.arange(nb), q_blocks))
    y = jnp.moveaxis(o, 0, 1).reshape(b, s, d) @ w_o
    if not with_ctx_out:
        return y, None
    s_cc = jnp.einsum('bqkgd,bckd->bkgqc', qc, kc).astype(jnp.float32) * scale
    p_cc = softmax_with_sink(s_cc).astype(vc.dtype)
    yc = jnp.einsum('bkgqc,bckd->bqkgd', p_cc, vc).reshape(b, -1, d) @ w_o
    return y, yc


def centred_mean_pool(u, window):
    s = u.shape[1]
    cs = jnp.pad(jnp.cumsum(u.astype(jnp.float32), axis=1), ((0, 0), (1, 0), (0, 0)))
    t = jnp.arange(s)
    lo = jnp.clip(t - window // 2, 0, s)
    hi = jnp.clip(t + window - window // 2, 0, s)
    total = jnp.take(cs, hi, axis=1) - jnp.take(cs, lo, axis=1)
    count = (hi - lo).astype(jnp.float32)
    return (total / count[None, :, None]).astype(u.dtype)


def multiscale_pool_mixer(u, w_pool, ch_scale):
    b, s, d = u.shape
    groups = jnp.split(u, N_POOL_GROUPS, axis=-1)
    diffs = jnp.stack([centred_mean_pool(gx, w) - gx for gx, w in zip(groups, POOL_WINDOWS)], axis=2)
    y = jnp.einsum('bsgc,gcd->bsgd', diffs, w_pool).reshape(b, s, d)
    return y * ch_scale


def short_conv(u, w, bias):
    d = u.shape[-1]
    pad = (CONV_WIDTH - 1) // 2
    y = lax.conv_general_dilated(u, w[:, None, :].astype(u.dtype), window_strides=(1,),
                                 padding=[(pad, CONV_WIDTH - 1 - pad)],
                                 dimension_numbers=('NWC', 'WIO', 'NWC'), feature_group_count=d)
    return y + bias


def gated_short_conv_mixer(u, w_in, conv_w, conv_b, w_out):
    b_gate, c_gate, x_in = jnp.split(u @ w_in, 3, axis=-1)
    return (b_gate * short_conv(c_gate * x_in, conv_w, conv_b)) @ w_out


def sqrelu_mlp(u, w1, w2):
    return jnp.square(jax.nn.relu(u @ w1)) @ w2


def setup_inputs(seed: int = 0) -> dict:
    key = jax.random.key(seed)
    ks = jax.random.split(key, 20)
    f32 = jnp.float32

    def nrm(k, shape, fan_in, gain=1.0):
        return jax.random.normal(k, shape, f32) * (gain * fan_in ** -0.5)

    qkv_width = (N_HEADS + 2 * N_KV_HEADS) * HEAD_DIM
    return {
        'x': jax.random.normal(ks[0], (BATCH, SEQ, D_MODEL), f32),
        'c': jax.random.normal(ks[1], (BATCH, D_MODEL), f32),
        'ctx': jax.random.normal(ks[2], (BATCH, CTX_LEN, D_MODEL), f32),
        'c_ctx': jax.random.normal(ks[3], (D_MODEL,), f32),
        'mod_w_down': nrm(ks[4], (DEPTH, D_MODEL, ADALN_RANK), D_MODEL),
        'mod_w_up': nrm(ks[5], (DEPTH, ADALN_RANK, 6 * D_MODEL), ADALN_RANK),
        'mod_b': 0.02 * jax.random.normal(ks[6], (DEPTH, 6 * D_MODEL), f32),
        'ln_g': 1.0 + 0.02 * jax.random.normal(ks[7], (DEPTH, 2, D_MODEL), f32),
        'ln_b': 0.02 * jax.random.normal(ks[8], (DEPTH, 2, D_MODEL), f32),
        'mlp_w1': nrm(ks[9], (DEPTH, D_MODEL, D_FF), D_MODEL),
        'mlp_w2': nrm(ks[10], (DEPTH, D_FF, D_MODEL), D_FF, DN_BETA),
        'attn_w_qkv': nrm(ks[11], (N_ATTN, D_MODEL, qkv_width), D_MODEL),
        'attn_w_o': nrm(ks[12], (N_ATTN, D_MODEL, D_MODEL), D_MODEL, DN_BETA),
        'attn_sink': 0.5 * jax.random.normal(ks[13], (N_ATTN, N_HEADS), f32),
        'pool_w': nrm(ks[14], (N_POOL, N_POOL_GROUPS, POOL_GROUP, POOL_GROUP), POOL_GROUP, DN_BETA),
        'pool_scale': 1.0 + 0.1 * jax.random.normal(ks[15], (N_POOL, D_MODEL), f32),
        'conv_w_in': nrm(ks[16], (N_CONV, D_MODEL, 3 * D_MODEL), D_MODEL),
        'conv_w': nrm(ks[17], (N_CONV, CONV_WIDTH, D_MODEL), CONV_WIDTH),
        'conv_b': 0.02 * jax.random.normal(ks[18], (N_CONV, D_MODEL), f32),
        'conv_w_out': nrm(ks[19], (N_CONV, D_MODEL, D_MODEL), D_MODEL, DN_BETA),
    }


def reference(x, c, ctx, c_ctx, mod_w_down, mod_w_up, mod_b, ln_g, ln_b, mlp_w1, mlp_w2,
              attn_w_qkv, attn_w_o, attn_sink, pool_w, pool_scale,
              conv_w_in, conv_w, conv_b, conv_w_out):
    cos, sin = axial_rope_tables(x.shape[1])
    for i in range(DEPTH):
        last = i == DEPTH - 1
        kind, j = i % N_MIXERS, i // N_MIXERS
        sh1, sc1, g1, sh2, sc2, g2 = [m[:, None, :] for m in adaln(c, mod_w_down[i], mod_w_up[i], mod_b[i])]
        sh1c, sc1c, g1c, sh2c, sc2c, g2c = adaln(c_ctx, mod_w_down[i], mod_w_up[i], mod_b[i])
        h = x * (1 + sc1) + sh1
        hc = ctx * (1 + sc1c) + sh1c
        if kind == 0:
            y, yc = windowed_gqa_attention(h, hc, attn_w_qkv[j], attn_w_o[j], attn_sink[j], cos, sin, not last)
        elif kind == 1:
            y = multiscale_pool_mixer(h, pool_w[j], pool_scale[j])
            yc = None if last else multiscale_pool_mixer(hc, pool_w[j], pool_scale[j])
        else:
            y = gated_short_conv_mixer(h, conv_w_in[j], conv_w[j], conv_b[j], conv_w_out[j])
            yc = None if last else gated_short_conv_mixer(hc, conv_w_in[j], conv_w[j], conv_b[j], conv_w_out[j])
        x = layer_norm(DN_ALPHA * x + g1 * y, ln_g[i, 0], ln_b[i, 0])
        h = x * (1 + sc2) + sh2
        x = layer_norm(DN_ALPHA * x + g2 * sqrelu_mlp(h, mlp_w1[i], mlp_w2[i]), ln_g[i, 1], ln_b[i, 1])
        if not last:
            ctx = layer_norm(DN_ALPHA * ctx + g1c * yc, ln_g[i, 0], ln_b[i, 0])
            hc = ctx * (1 + sc2c) + sh2c
            ctx = layer_norm(DN_ALPHA * ctx + g2c * sqrelu_mlp(hc, mlp_w1[i], mlp_w2[i]), ln_g[i, 1], ln_b[i, 1])
    return x
```

```python
import functools

import jax
import jax.numpy as jnp
from jax import lax
from jax.experimental import pallas as pl
from jax.experimental.pallas import tpu as pltpu

HEAD_DIM = 128
KV_GROUP = 4
GRID_W = 64
ATT_BLOCK = 128
ROPE_THETA = 10000.0
POOL_WINDOWS = (2, 4, 8, 16)
CONV_WIDTH = 3
N_MIXERS = 3
LN_EPS = 1e-5

LANES = 128
ROW_TILE = 256
HALO = 16
VMEM_LIMIT = 56 * 1024 * 1024

F32 = jnp.float32
BF16 = jnp.bfloat16


def _pick(n, candidates):
    for c in candidates:
        if n % c == 0:
            return c
    raise ValueError(f"no tile in {candidates} divides {n}")


def _params(*sem):
    return pltpu.CompilerParams(dimension_semantics=sem, vmem_limit_bytes=VMEM_LIMIT)


def _vecmat_body(a_ref, w_ref, o_ref, *, silu):
    a = a_ref[0]
    if silu:
        a = a * jax.nn.sigmoid(a)
    o_ref[0] = jnp.dot(a.astype(BF16), w_ref[0].astype(BF16), preferred_element_type=F32)


def _vecmat_bias_body(a_ref, w_ref, b_ref, o_ref):
    o_ref[0] = jnp.dot(a_ref[0].astype(BF16), w_ref[0].astype(BF16),
                       preferred_element_type=F32) + b_ref[0]


def _vecmat(a, w, bias=None, *, silu=False):
    nl, rows, k = a.shape
    n = w.shape[2]
    bn = _pick(n, (2048, 1024, 512, 256, 128))
    in_specs = [pl.BlockSpec((1, rows, k), lambda l, j: (l, 0, 0)),
                pl.BlockSpec((1, k, bn), lambda l, j: (l, 0, j))]
    args = [a, w]
    if bias is None:
        body = functools.partial(_vecmat_body, silu=silu)
    else:
        body = _vecmat_bias_body
        in_specs.append(pl.BlockSpec((1, 1, bn), lambda l, j: (l, 0, j)))
        args.append(bias.reshape(nl, 1, n))
    return pl.pallas_call(
        body, grid=(nl, n // bn), in_specs=in_specs,
        out_specs=pl.BlockSpec((1, rows, bn), lambda l, j: (l, 0, j)),
        out_shape=jax.ShapeDtypeStruct((nl, rows, n), F32),
        compiler_params=_params("parallel", "parallel"))(*args)


def _prep_body(x_ref, c_ref, m_ref, xo_ref, ho_ref, *, n_x_tiles):
    is_ctx = pl.program_id(0) >= n_x_tiles
    xv = jnp.where(is_ctx, c_ref[0], x_ref[0])
    xo_ref[...] = xv
    ho_ref[...] = (xv * (1.0 + m_ref[0, 0, 1:2, :]) + m_ref[0, 0, 0:1, :]).astype(BF16)


def _prep(x, ctx, mods):
    _, s, d = x.shape
    c = ctx.shape[1]
    t = s + c
    rt = ROW_TILE
    nx = s // rt
    return pl.pallas_call(
        functools.partial(_prep_body, n_x_tiles=nx), grid=(t // rt,),
        in_specs=[pl.BlockSpec((1, rt, d), lambda i: (0, jnp.minimum(i, nx - 1), 0)),
                  pl.BlockSpec((1, rt, d), lambda i: (0, jnp.maximum(i - nx, 0), 0)),
                  pl.BlockSpec((1, 1, 6, d), lambda i: (0, (i >= nx).astype(jnp.int32), 0, 0))],
        out_specs=[pl.BlockSpec((rt, d), lambda i: (i, 0)), pl.BlockSpec((rt, d), lambda i: (i, 0))],
        out_shape=[jax.ShapeDtypeStruct((t, d), F32), jax.ShapeDtypeStruct((t, d), BF16)],
        compiler_params=_params("parallel"))(x, ctx, mods)


def _ln_body(x_ref, y_ref, mg_ref, mn_ref, g_ref, b_ref, xo_ref, *ho_ref, gate, shift, scale, which, alpha):
    z = alpha * x_ref[...] + mg_ref[0, 0, gate:gate + 1, :] * y_ref[...]
    mu = jnp.mean(z, axis=-1, keepdims=True)
    zc = z - mu
    var = jnp.mean(zc * zc, axis=-1, keepdims=True)
    xn = zc * lax.rsqrt(var + LN_EPS) * g_ref[0, which:which + 1, :] + b_ref[0, which:which + 1, :]
    xo_ref[...] = xn
    if ho_ref:
        ho_ref[0][...] = (xn * (1.0 + mn_ref[0, 0, scale:scale + 1, :])
                          + mn_ref[0, 0, shift:shift + 1, :]).astype(BF16)


def _ln(x, y, mods, ln_g, ln_b, *, layer, which, next_layer, n_x_rows, rows, alpha, emit_h):
    d = x.shape[1]
    rt = ROW_TILE
    nx = n_x_rows // rt
    gate = 2 + 3 * which
    shift, scale = (3, 4) if which == 0 else (0, 1)
    is_ctx = lambda i: (i >= nx).astype(jnp.int32)
    row = pl.BlockSpec((rt, d), lambda i: (i, 0))
    out_specs = [row]
    out_shape = [jax.ShapeDtypeStruct((rows, d), F32)]
    if emit_h:
        out_specs.append(row)
        out_shape.append(jax.ShapeDtypeStruct((rows, d), BF16))
    body = functools.partial(_ln_body, gate=gate, shift=shift, scale=scale, which=which, alpha=alpha)
    out = pl.pallas_call(
        body, grid=(rows // rt,),
        in_specs=[row, row,
                  pl.BlockSpec((1, 1, 6, d), lambda i: (layer, is_ctx(i), 0, 0)),
                  pl.BlockSpec((1, 1, 6, d), lambda i: (next_layer, is_ctx(i), 0, 0)),
                  pl.BlockSpec((1, 2, d), lambda i: (layer, 0, 0)),
                  pl.BlockSpec((1, 2, d), lambda i: (layer, 0, 0))],
        out_specs=out_specs, out_shape=out_shape,
        compiler_params=_params("parallel"))(x, y, mods, mods, ln_g, ln_b)
    return (out[0], out[1]) if emit_h else (out[0], None)


def _seq_bounds(i, rt, n_x_rows, n_rows):
    in_x = i * rt < n_x_rows
    return jnp.where(in_x, 0, n_x_rows), jnp.where(in_x, n_x_rows, n_rows)


def _pooldiff_body(xp_ref, x_ref, xn_ref, m_ref, o_ref, *, n_x_rows, n_rows, group):
    rt = x_ref.shape[0]
    i = pl.program_id(0)
    lo, hi = _seq_bounds(i, rt, n_x_rows, n_rows)
    r = i * rt + lax.broadcasted_iota(jnp.int32, (rt, 1), 0)
    xx = jnp.concatenate([xp_ref[...], x_ref[...], xn_ref[...]], axis=0)
    hh = xx * (1.0 + m_ref[0, 0, 1:2, :]) + m_ref[0, 0, 0:1, :]
    for g, w in enumerate(POOL_WINDOWS):
        hg = hh[:, g * group:(g + 1) * group]
        tot = jnp.zeros((rt, group), F32)
        cnt = jnp.zeros((rt, 1), F32)
        for off in range(-(w // 2), w - w // 2):
            ok = (r + off >= lo) & (r + off < hi)
            tot = tot + jnp.where(ok, hg[HALO + off:HALO + off + rt], 0.0)
            cnt = cnt + ok.astype(F32)
        o_ref[:, g * group:(g + 1) * group] = (tot * (1.0 / cnt) - hg[HALO:HALO + rt]).astype(BF16)


def _halo_specs(rt, d, n_rows):
    per = rt // HALO
    last = n_rows // HALO - 1
    return (pl.BlockSpec((HALO, d), lambda i: (jnp.maximum(i * per - 1, 0), 0)),
            pl.BlockSpec((rt, d), lambda i: (i, 0)),
            pl.BlockSpec((HALO, d), lambda i: (jnp.minimum((i + 1) * per, last), 0)))


def _pooldiff(x, mods, *, layer, n_x_rows):
    t, d = x.shape
    rt = ROW_TILE
    nx = n_x_rows // rt
    prev, cur, nxt = _halo_specs(rt, d, t)
    body = functools.partial(_pooldiff_body, n_x_rows=n_x_rows, n_rows=t, group=d // len(POOL_WINDOWS))
    return pl.pallas_call(
        body, grid=(t // rt,),
        in_specs=[prev, cur, nxt,
                  pl.BlockSpec((1, 1, 6, d), lambda i: (layer, (i >= nx).astype(jnp.int32), 0, 0))],
        out_specs=pl.BlockSpec((rt, d), lambda i: (i, 0)),
        out_shape=jax.ShapeDtypeStruct((t, d), BF16),
        compiler_params=_params("parallel"))(x, x, x, mods)


def _conv_body(b_ref, vp_ref, v_ref, vn_ref, w_ref, cb_ref, o_ref, *, n_x_rows, n_rows):
    rt = v_ref.shape[0]
    i = pl.program_id(0)
    lo, hi = _seq_bounds(i, rt, n_x_rows, n_rows)
    r = i * rt + lax.broadcasted_iota(jnp.int32, (rt, 1), 0)
    vv = jnp.concatenate([vp_ref[...], v_ref[...], vn_ref[...]], axis=0).astype(F32)
    pad = (CONV_WIDTH - 1) // 2
    acc = jnp.zeros(v_ref.shape, F32)
    for k in range(CONV_WIDTH):
        off = k - pad
        ok = (r + off >= lo) & (r + off < hi)
        acc = acc + jnp.where(ok, vv[HALO + off:HALO + off + rt], 0.0) * w_ref[0, k:k + 1, :]
    o_ref[...] = (b_ref[...].astype(F32) * (acc + cb_ref[0])).astype(BF16)


def _gated_conv(b, v, conv_w, conv_b, *, layer, n_x_rows):
    t, d = v.shape
    rt = ROW_TILE
    prev, cur, nxt = _halo_specs(rt, d, t)
    body = functools.partial(_conv_body, n_x_rows=n_x_rows, n_rows=t)
    return pl.pallas_call(
        body, grid=(t // rt,),
        in_specs=[cur, prev, cur, nxt,
                  pl.BlockSpec((1, CONV_WIDTH, d), lambda i: (layer, 0, 0)),
                  pl.BlockSpec((1, 1, d), lambda i: (layer, 0, 0))],
        out_specs=pl.BlockSpec((rt, d), lambda i: (i, 0)),
        out_shape=jax.ShapeDtypeStruct((t, d), BF16),
        compiler_params=_params("parallel"))(b, v, v, v, conv_w, conv_b.reshape(-1, 1, d))


def _mm_tiles(m, n):
    return _pick(m, (768, 640, 512, 256, 128)), _pick(n, (1024, 512, 256, 128))


def _mm_body(a_ref, w_ref, o_ref):
    o_ref[...] = jnp.dot(a_ref[...], w_ref[...], preferred_element_type=F32).astype(o_ref.dtype)


def _matmul(a, w, out_dtype):
    m, k = a.shape
    n = w.shape[1]
    bm, bn = _mm_tiles(m, n)
    return pl.pallas_call(
        _mm_body, grid=(m // bm, n // bn),
        in_specs=[pl.BlockSpec((bm, k), lambda i, j: (i, 0)), pl.BlockSpec((k, bn), lambda i, j: (0, j))],
        out_specs=pl.BlockSpec((bm, bn), lambda i, j: (i, j)),
        out_shape=jax.ShapeDtypeStruct((m, n), out_dtype),
        compiler_params=_params("parallel", "parallel"))(a, w)


def _mm_rope_body(a_ref, w_ref, c_ref, s_ref, o_ref, *, n_rope_tiles):
    j = pl.program_id(1)
    acc = jnp.dot(a_ref[...], w_ref[...], preferred_element_type=F32)

    @pl.when(j < n_rope_tiles)
    def _():
        bn = acc.shape[1]
        lane = lax.broadcasted_iota(jnp.int32, acc.shape, 1)
        quarter = HEAD_DIM // 4
        partner = jnp.where((lane & quarter) == 0,
                            pltpu.roll(acc, bn - quarter, 1), pltpu.roll(acc, quarter, 1))
        reps = bn // HEAD_DIM
        cos = jnp.concatenate([c_ref[...]] * reps, axis=1)
        sin = jnp.concatenate([s_ref[...]] * reps, axis=1)
        o_ref[...] = (acc * cos + partner * sin).astype(o_ref.dtype)

    @pl.when(j >= n_rope_tiles)
    def _():
        o_ref[...] = acc.astype(o_ref.dtype)


def _matmul_rope(a, w, cos, sin, n_rope_cols):
    m, k = a.shape
    n = w.shape[1]
    bm, _ = _mm_tiles(m, n)
    bn = _pick(n - n_rope_cols, (1024, 512, 256, 128))
    assert n_rope_cols % bn == 0
    body = functools.partial(_mm_rope_body, n_rope_tiles=n_rope_cols // bn)
    return pl.pallas_call(
        body, grid=(m // bm, n // bn),
        in_specs=[pl.BlockSpec((bm, k), lambda i, j: (i, 0)), pl.BlockSpec((k, bn), lambda i, j: (0, j)),
                  pl.BlockSpec((bm, HEAD_DIM), lambda i, j: (i, 0)),
                  pl.BlockSpec((bm, HEAD_DIM), lambda i, j: (i, 0))],
        out_specs=pl.BlockSpec((bm, bn), lambda i, j: (i, j)),
        out_shape=jax.ShapeDtypeStruct((m, n), BF16),
        compiler_params=_params("parallel", "parallel"))(a, w, cos, sin)


def _mm_group_body(a_ref, w_ref, s_ref, o_ref):
    o_ref[...] = jnp.dot(a_ref[...], w_ref[0], preferred_element_type=F32) * s_ref[...]


def _matmul_groups(a, w, ch_scale):
    m, d = a.shape
    ng, gk, gn = w.shape
    bm = _pick(m, (768, 640, 512, 256, 128))
    return pl.pallas_call(
        _mm_group_body, grid=(m // bm, ng),
        in_specs=[pl.BlockSpec((bm, gk), lambda i, g: (i, g)), pl.BlockSpec((1, gk, gn), lambda i, g: (g, 0, 0)),
                  pl.BlockSpec((1, gn), lambda i, g: (0, g))],
        out_specs=pl.BlockSpec((bm, gn), lambda i, g: (i, g)),
        out_shape=jax.ShapeDtypeStruct((m, ng * gn), F32),
        compiler_params=_params("parallel", "parallel"))(a, w, ch_scale.reshape(1, -1))


def _mm_gate_body(a_ref, wb_ref, wc_ref, wx_ref, b_ref, v_ref):
    a = a_ref[...]
    b_ref[...] = jnp.dot(a, wb_ref[...], preferred_element_type=F32).astype(BF16)
    v_ref[...] = (jnp.dot(a, wc_ref[...], preferred_element_type=F32)
                  * jnp.dot(a, wx_ref[...], preferred_element_type=F32)).astype(BF16)


def _matmul_gates(a, w_in):
    m, k = a.shape
    d = w_in.shape[1] // 3
    bm = _pick(m, (768, 640, 512, 256, 128))
    bn = _pick(d, (512, 256, 128))
    nj = d // bn
    a_spec = pl.BlockSpec((bm, k), lambda i, j: (i, 0))
    out = pl.BlockSpec((bm, bn), lambda i, j: (i, j))
    return pl.pallas_call(
        _mm_gate_body, grid=(m // bm, nj),
        in_specs=[a_spec,
                  pl.BlockSpec((k, bn), lambda i, j: (0, j)),
                  pl.BlockSpec((k, bn), lambda i, j: (0, j + nj)),
                  pl.BlockSpec((k, bn), lambda i, j: (0, j + 2 * nj))],
        out_specs=[out, out],
        out_shape=[jax.ShapeDtypeStruct((m, d), BF16), jax.ShapeDtypeStruct((m, d), BF16)],
        compiler_params=_params("parallel", "parallel"))(a, w_in, w_in, w_in)


def _mlp_body(h_ref, w1_ref, w2_ref, o_ref):
    @pl.when(pl.program_id(1) == 0)
    def _():
        o_ref[...] = jnp.zeros(o_ref.shape, F32)

    u = jnp.maximum(jnp.dot(h_ref[...], w1_ref[...], preferred_element_type=F32), 0.0)
    o_ref[...] += jnp.dot((u * u).astype(BF16), w2_ref[...], preferred_element_type=F32)


def _mlp(h, w1, w2):
    m, d = h.shape
    f = w1.shape[1]
    bm = _pick(m, (768, 640, 512, 256, 128))
    bf = _pick(f, (512, 256, 128))
    return pl.pallas_call(
        _mlp_body, grid=(m // bm, f // bf),
        in_specs=[pl.BlockSpec((bm, d), lambda i, j: (i, 0), pipeline_mode=pl.Buffered(1)),
                  pl.BlockSpec((d, bf), lambda i, j: (0, j)),
                  pl.BlockSpec((bf, d), lambda i, j: (j, 0))],
        out_specs=pl.BlockSpec((bm, d), lambda i, j: (i, 0)),
        out_shape=jax.ShapeDtypeStruct((m, d), F32),
        compiler_params=_params("parallel", "arbitrary"))(h, w1, w2)


def _attn_body(sink_ref, q_ref, kp_ref, kc_ref, kn_ref, kx_ref, vp_ref, vc_ref, vn_ref, vx_ref, o_ref,
               *, n_x_blocks, n_x_rows):
    kh = pl.program_id(0)
    n = pl.program_id(1)
    blk = ATT_BLOCK
    q = q_ref[...]
    q4 = jnp.concatenate([q[:, g * HEAD_DIM:(g + 1) * HEAD_DIM] for g in range(KV_GROUP)], axis=0)
    kcat = jnp.concatenate([kp_ref[...], kc_ref[...], kn_ref[...], kx_ref[...]], axis=0)
    vcat = jnp.concatenate([vp_ref[...], vc_ref[...], vn_ref[...], vx_ref[...]], axis=0)
    nk = kcat.shape[0]
    s = lax.dot_general(q4, kcat, (((1,), (1,)), ((), ())), preferred_element_type=F32)
    s = s * (HEAD_DIM ** -0.5)

    qi = lax.broadcasted_iota(jnp.int32, (blk, nk), 0)
    kj = lax.broadcasted_iota(jnp.int32, (blk, nk), 1)
    rel = kj - blk - qi
    kpos = kj + (n - 1) * blk
    kend = jnp.where(n < n_x_blocks, n_x_rows, -1)
    valid = (kj >= 3 * blk) | ((rel <= blk) & (rel >= -blk) & (kpos >= 0) & (kpos < kend))

    for g in range(KV_GROUP):
        sg = jnp.where(valid, s[g * blk:(g + 1) * blk], -jnp.inf)
        sk = sink_ref[kh * KV_GROUP + g]
        mx = jnp.maximum(jnp.max(sg, axis=1, keepdims=True), sk)
        p = jnp.exp(sg - mx)
        den = jnp.sum(p, axis=1, keepdims=True) + jnp.exp(sk - mx)
        og = jnp.dot(p.astype(BF16), vcat, preferred_element_type=F32)
        o_ref[:, g * HEAD_DIM:(g + 1) * HEAD_DIM] = (og / den).astype(o_ref.dtype)


def _attention(qkv, sink, *, n_x_rows, n_heads):
    t = qkv.shape[0]
    c = t - n_x_rows
    blk = ATT_BLOCK
    assert n_x_rows % c == 0 and c % blk == 0
    nkv = n_heads // KV_GROUP
    nbx = n_x_rows // blk
    kcol = n_heads
    vcol = n_heads + nkv
    clampx = lambda b: jnp.clip(b, 0, nbx - 1)

    def band(col, shift):
        return pl.BlockSpec((blk, HEAD_DIM), lambda kh, n: (clampx(n + shift), col + kh))

    def ctx(col):
        return pl.BlockSpec((c, HEAD_DIM), lambda kh, n: (n_x_rows // c, col + kh))

    body = functools.partial(_attn_body, n_x_blocks=nbx, n_x_rows=n_x_rows)
    return pl.pallas_call(
        body, grid=(nkv, t // blk),
        in_specs=[pl.BlockSpec(memory_space=pltpu.SMEM),
                  pl.BlockSpec((blk, KV_GROUP * HEAD_DIM), lambda kh, n: (n, kh)),
                  band(kcol, -1), band(kcol, 0), band(kcol, 1), ctx(kcol),
                  band(vcol, -1), band(vcol, 0), band(vcol, 1), ctx(vcol)],
        out_specs=pl.BlockSpec((blk, KV_GROUP * HEAD_DIM), lambda kh, n: (n, kh)),
        out_shape=jax.ShapeDtypeStruct((t, n_heads * HEAD_DIM), BF16),
        compiler_params=_params("parallel", "parallel"))(sink, qkv, *([qkv] * 8))


def _rope_tables(n_x_rows, n_rows):
    tpos = jnp.arange(n_x_rows)
    n_freq = HEAD_DIM // 4
    freqs = ROPE_THETA ** (-jnp.arange(n_freq, dtype=F32) / n_freq)
    ang_r = (tpos // GRID_W).astype(F32)[:, None] * freqs
    ang_c = (tpos % GRID_W).astype(F32)[:, None] * freqs
    cos = jnp.concatenate([jnp.cos(ang_r)] * 2 + [jnp.cos(ang_c)] * 2, axis=1)
    sin = jnp.concatenate([-jnp.sin(ang_r), jnp.sin(ang_r), -jnp.sin(ang_c), jnp.sin(ang_c)], axis=1)
    pad = n_rows - n_x_rows
    cos = jnp.concatenate([cos, jnp.ones((pad, HEAD_DIM), F32)], axis=0)
    sin = jnp.concatenate([sin, jnp.zeros((pad, HEAD_DIM), F32)], axis=0)
    return cos, sin


def kernel(x, c, ctx, c_ctx, mod_w_down, mod_w_up, mod_b, ln_g, ln_b, mlp_w1, mlp_w2,
           attn_w_qkv, attn_w_o, attn_sink, pool_w, pool_scale,
           conv_w_in, conv_w, conv_b, conv_w_out):
    batch, s, d = x.shape
    assert batch == 1 and s % ROW_TILE == 0 and ctx.shape[1] % ROW_TILE == 0
    depth = mod_w_down.shape[0]
    t = s + ctx.shape[1]
    n_heads = d // HEAD_DIM
    alpha = (2 * depth) ** 0.25

    cond = jnp.zeros((8, d), F32).at[0].set(c[0]).at[1].set(c_ctx)
    cond = jnp.broadcast_to(cond, (depth, 8, d))
    mods = _vecmat(_vecmat(cond, mod_w_down, silu=True), mod_w_up, mod_b)
    mods = mods[:, :2].reshape(depth, 2, 6, d)

    cos, sin = _rope_tables(s, t)
    xs, h = _prep(x, ctx, mods)

    for i in range(depth):
        last = i == depth - 1
        kind, j = i % N_MIXERS, i // N_MIXERS
        if kind == 0:
            w_qkv = attn_w_qkv[j].astype(BF16)
            n_rope = (n_heads + n_heads // KV_GROUP) * HEAD_DIM
            qkv = _matmul_rope(h, w_qkv, cos, sin, n_rope)
            o = _attention(qkv, attn_sink[j], n_x_rows=s, n_heads=n_heads)
            y = _matmul(o, attn_w_o[j].astype(BF16), F32)
        elif kind == 1:
            diffs = _pooldiff(xs, mods, layer=i, n_x_rows=s)
            y = _matmul_groups(diffs, pool_w[j].astype(BF16), pool_scale[j])
        else:
            b, v = _matmul_gates(h, conv_w_in[j].astype(BF16))
            g = _gated_conv(b, v, conv_w, conv_b, layer=j, n_x_rows=s)
            y = _matmul(g, conv_w_out[j].astype(BF16), F32)
        xs, h = _ln(xs, y, mods, ln_g, ln_b, layer=i, which=0, next_layer=i, n_x_rows=s, rows=t,
                    alpha=alpha, emit_h=True)
        y = _mlp(h, mlp_w1[i].astype(BF16), mlp_w2[i].astype(BF16))
        need_h = (not last) and (i + 1) % N_MIXERS != 1
        xs, h = _ln(xs, y, mods, ln_g, ln_b, layer=i, which=1, next_layer=min(i + 1, depth - 1),
                    n_x_rows=s, rows=s if last else t, alpha=alpha, emit_h=need_h)
    return xs.reshape(batch, s, d)
```

```python
import functools

import jax
import jax.numpy as jnp
from jax import lax
from jax.experimental import pallas as pl
from jax.experimental.pallas import tpu as pltpu

HEAD_DIM = 128
KV_GROUP = 4
GRID_W = 64
ATT_BLOCK = 128
ROPE_THETA = 10000.0
POOL_WINDOWS = (2, 4, 8, 16)
CONV_WIDTH = 3
N_MIXERS = 3
LN_EPS = 1e-5

LANES = 128
ROW_TILE = 256
HALO = 16
VMEM_LIMIT = 56 * 1024 * 1024

F32 = jnp.float32
BF16 = jnp.bfloat16


def _pick(n, candidates):
    for c in candidates:
        if n % c == 0:
            return c
    raise ValueError(f"no tile in {candidates} divides {n}")


def _params(*sem):
    return pltpu.CompilerParams(dimension_semantics=sem, vmem_limit_bytes=VMEM_LIMIT)


def _vecmat_body(a_ref, w_ref, o_ref, *, silu):
    a = a_ref[0]
    if silu:
        a = a * jax.nn.sigmoid(a)
    o_ref[0] = jnp.dot(a.astype(BF16), w_ref[0].astype(BF16), preferred_element_type=F32)


def _vecmat_bias_body(a_ref, w_ref, b_ref, o_ref):
    o_ref[0] = jnp.dot(a_ref[0].astype(BF16), w_ref[0].astype(BF16),
                       preferred_element_type=F32) + b_ref[0]


def _vecmat(a, w, bias=None, *, silu=False):
    nl, rows, k = a.shape
    n = w.shape[2]
    bn = _pick(n, (2048, 1024, 512, 256, 128))
    in_specs = [pl.BlockSpec((1, rows, k), lambda l, j: (l, 0, 0)),
                pl.BlockSpec((1, k, bn), lambda l, j: (l, 0, j))]
    args = [a, w]
    if bias is None:
        body = functools.partial(_vecmat_body, silu=silu)
    else:
        body = _vecmat_bias_body
        in_specs.append(pl.BlockSpec((1, 1, bn), lambda l, j: (l, 0, j)))
        args.append(bias.reshape(nl, 1, n))
    return pl.pallas_call(
        body, grid=(nl, n // bn), in_specs=in_specs,
        out_specs=pl.BlockSpec((1, rows, bn), lambda l, j: (l, 0, j)),
        out_shape=jax.ShapeDtypeStruct((nl, rows, n), F32), name="adaln_vecmat",
        compiler_params=_params("parallel", "parallel"))(*args)


def _prep_body(x_ref, c_ref, m_ref, xo_ref, ho_ref, *, n_x_tiles):
    is_ctx = pl.program_id(0) >= n_x_tiles
    xv = jnp.where(is_ctx, c_ref[0], x_ref[0])
    xo_ref[...] = xv
    ho_ref[...] = (xv * (1.0 + m_ref[0, 0, 1:2, :]) + m_ref[0, 0, 0:1, :]).astype(BF16)


def _prep(x, ctx, mods):
    _, s, d = x.shape
    c = ctx.shape[1]
    t = s + c
    rt = ROW_TILE
    nx = s // rt
    return pl.pallas_call(
        functools.partial(_prep_body, n_x_tiles=nx), grid=(t // rt,),
        in_specs=[pl.BlockSpec((1, rt, d), lambda i: (0, jnp.minimum(i, nx - 1), 0)),
                  pl.BlockSpec((1, rt, d), lambda i: (0, jnp.maximum(i - nx, 0), 0)),
                  pl.BlockSpec((1, 1, 6, d), lambda i: (0, (i >= nx).astype(jnp.int32), 0, 0))],
        out_specs=[pl.BlockSpec((rt, d), lambda i: (i, 0)), pl.BlockSpec((rt, d), lambda i: (i, 0))],
        out_shape=[jax.ShapeDtypeStruct((t, d), F32), jax.ShapeDtypeStruct((t, d), BF16)], name="prep",
        compiler_params=_params("parallel"))(x, ctx, mods)


def _ln_body(x_ref, y_ref, mg_ref, mn_ref, g_ref, b_ref, xo_ref, *ho_ref, gate, shift, scale, which, alpha):
    z = alpha * x_ref[...] + mg_ref[0, 0, gate:gate + 1, :] * y_ref[...]
    mu = jnp.mean(z, axis=-1, keepdims=True)
    zc = z - mu
    var = jnp.mean(zc * zc, axis=-1, keepdims=True)
    xn = zc * lax.rsqrt(var + LN_EPS) * g_ref[0, which:which + 1, :] + b_ref[0, which:which + 1, :]
    xo_ref[...] = xn
    if ho_ref:
        ho_ref[0][...] = (xn * (1.0 + mn_ref[0, 0, scale:scale + 1, :])
                          + mn_ref[0, 0, shift:shift + 1, :]).astype(BF16)


def _ln(x, y, mods, ln_g, ln_b, *, layer, which, next_layer, n_x_rows, rows, alpha, emit_h):
    d = x.shape[1]
    rt = ROW_TILE
    nx = n_x_rows // rt
    gate = 2 + 3 * which
    shift, scale = (3, 4) if which == 0 else (0, 1)
    is_ctx = lambda i: (i >= nx).astype(jnp.int32)
    row = pl.BlockSpec((rt, d), lambda i: (i, 0))
    out_specs = [row]
    out_shape = [jax.ShapeDtypeStruct((rows, d), F32)]
    if emit_h:
        out_specs.append(row)
        out_shape.append(jax.ShapeDtypeStruct((rows, d), BF16))
    body = functools.partial(_ln_body, gate=gate, shift=shift, scale=scale, which=which, alpha=alpha)
    out = pl.pallas_call(
        body, grid=(rows // rt,),
        in_specs=[row, row,
                  pl.BlockSpec((1, 1, 6, d), lambda i: (layer, is_ctx(i), 0, 0)),
                  pl.BlockSpec((1, 1, 6, d), lambda i: (next_layer, is_ctx(i), 0, 0)),
                  pl.BlockSpec((1, 2, d), lambda i: (layer, 0, 0)),
                  pl.BlockSpec((1, 2, d), lambda i: (layer, 0, 0))],
        out_specs=out_specs, out_shape=out_shape, name="resid_layernorm",
        compiler_params=_params("parallel"))(x, y, mods, mods, ln_g, ln_b)
    return (out[0], out[1]) if emit_h else (out[0], None)


def _seq_bounds(i, rt, n_x_rows, n_rows):
    in_x = i * rt < n_x_rows
    return jnp.where(in_x, 0, n_x_rows), jnp.where(in_x, n_x_rows, n_rows)


def _pooldiff_body(xp_ref, x_ref, xn_ref, m_ref, o_ref, *, n_x_rows, n_rows, group):
    rt = x_ref.shape[0]
    i = pl.program_id(0)
    lo, hi = _seq_bounds(i, rt, n_x_rows, n_rows)
    r = i * rt + lax.broadcasted_iota(jnp.int32, (rt, 1), 0)
    xx = jnp.concatenate([xp_ref[...], x_ref[...], xn_ref[...]], axis=0)
    hh = xx * (1.0 + m_ref[0, 0, 1:2, :]) + m_ref[0, 0, 0:1, :]
    for g, w in enumerate(POOL_WINDOWS):
        hg = hh[:, g * group:(g + 1) * group]
        tot = jnp.zeros((rt, group), F32)
        cnt = jnp.zeros((rt, 1), F32)
        for off in range(-(w // 2), w - w // 2):
            ok = (r + off >= lo) & (r + off < hi)
            tot = tot + jnp.where(ok, hg[HALO + off:HALO + off + rt], 0.0)
            cnt = cnt + ok.astype(F32)
        o_ref[:, g * group:(g + 1) * group] = (tot * (1.0 / cnt) - hg[HALO:HALO + rt]).astype(BF16)


def _halo_specs(rt, d, n_rows):
    per = rt // HALO
    last = n_rows // HALO - 1
    return (pl.BlockSpec((HALO, d), lambda i: (jnp.maximum(i * per - 1, 0), 0)),
            pl.BlockSpec((rt, d), lambda i: (i, 0)),
            pl.BlockSpec((HALO, d), lambda i: (jnp.minimum((i + 1) * per, last), 0)))


def _pooldiff(x, mods, *, layer, n_x_rows):
    t, d = x.shape
    rt = ROW_TILE
    nx = n_x_rows // rt
    prev, cur, nxt = _halo_specs(rt, d, t)
    body = functools.partial(_pooldiff_body, n_x_rows=n_x_rows, n_rows=t, group=d // len(POOL_WINDOWS))
    return pl.pallas_call(
        body, grid=(t // rt,),
        in_specs=[prev, cur, nxt,
                  pl.BlockSpec((1, 1, 6, d), lambda i: (layer, (i >= nx).astype(jnp.int32), 0, 0))],
        out_specs=pl.BlockSpec((rt, d), lambda i: (i, 0)),
        out_shape=jax.ShapeDtypeStruct((t, d), BF16), name="pool_diff",
        compiler_params=_params("parallel"))(x, x, x, mods)


def _conv_body(b_ref, vp_ref, v_ref, vn_ref, w_ref, cb_ref, o_ref, *, n_x_rows, n_rows):
    rt = v_ref.shape[0]
    i = pl.program_id(0)
    lo, hi = _seq_bounds(i, rt, n_x_rows, n_rows)
    r = i * rt + lax.broadcasted_iota(jnp.int32, (rt, 1), 0)
    vv = jnp.concatenate([vp_ref[...], v_ref[...], vn_ref[...]], axis=0).astype(F32)
    pad = (CONV_WIDTH - 1) // 2
    acc = jnp.zeros(v_ref.shape, F32)
    for k in range(CONV_WIDTH):
        off = k - pad
        ok = (r + off >= lo) & (r + off < hi)
        acc = acc + jnp.where(ok, vv[HALO + off:HALO + off + rt], 0.0) * w_ref[0, k:k + 1, :]
    o_ref[...] = (b_ref[...].astype(F32) * (acc + cb_ref[0])).astype(BF16)


def _gated_conv(b, v, conv_w, conv_b, *, layer, n_x_rows):
    t, d = v.shape
    rt = ROW_TILE
    prev, cur, nxt = _halo_specs(rt, d, t)
    body = functools.partial(_conv_body, n_x_rows=n_x_rows, n_rows=t)
    return pl.pallas_call(
        body, grid=(t // rt,),
        in_specs=[cur, prev, cur, nxt,
                  pl.BlockSpec((1, CONV_WIDTH, d), lambda i: (layer, 0, 0)),
                  pl.BlockSpec((1, 1, d), lambda i: (layer, 0, 0))],
        out_specs=pl.BlockSpec((rt, d), lambda i: (i, 0)),
        out_shape=jax.ShapeDtypeStruct((t, d), BF16), name="gated_conv",
        compiler_params=_params("parallel"))(b, v, v, v, conv_w, conv_b.reshape(-1, 1, d))


def _mm_tiles(m, n):
    return _pick(m, (768, 640, 512, 256, 128)), _pick(n, (1024, 512, 256, 128))


def _mm_body(a_ref, w_ref, o_ref):
    o_ref[...] = jnp.dot(a_ref[...], w_ref[...], preferred_element_type=F32).astype(o_ref.dtype)


def _matmul(a, w, out_dtype):
    m, k = a.shape
    n = w.shape[1]
    bm, bn = _mm_tiles(m, n)
    return pl.pallas_call(
        _mm_body, grid=(m // bm, n // bn),
        in_specs=[pl.BlockSpec((bm, k), lambda i, j: (i, 0)), pl.BlockSpec((k, bn), lambda i, j: (0, j))],
        out_specs=pl.BlockSpec((bm, bn), lambda i, j: (i, j)),
        out_shape=jax.ShapeDtypeStruct((m, n), out_dtype), name="matmul",
        compiler_params=_params("parallel", "parallel"))(a, w)


def _mm_rope_body(a_ref, w_ref, c_ref, s_ref, o_ref, *, n_rope_tiles):
    rotated = pl.program_id(1) < n_rope_tiles
    acc = jnp.dot(a_ref[...], w_ref[...], preferred_element_type=F32)
    bn = acc.shape[1]
    lane = lax.broadcasted_iota(jnp.int32, acc.shape, 1)
    quarter = HEAD_DIM // 4
    partner = jnp.where((lane & quarter) == 0,
                        pltpu.roll(acc, bn - quarter, 1), pltpu.roll(acc, quarter, 1))
    reps = bn // HEAD_DIM
    cos = jnp.concatenate([jnp.where(rotated, c_ref[...], 1.0)] * reps, axis=1)
    sin = jnp.concatenate([jnp.where(rotated, s_ref[...], 0.0)] * reps, axis=1)
    o_ref[...] = (acc * cos + partner * sin).astype(o_ref.dtype)


def _matmul_rope(a, w, cos, sin, n_rope_cols):
    m, k = a.shape
    n = w.shape[1]
    bm, _ = _mm_tiles(m, n)
    bn = _pick(n - n_rope_cols, (1024, 512, 256, 128))
    assert n_rope_cols % bn == 0
    body = functools.partial(_mm_rope_body, n_rope_tiles=n_rope_cols // bn)
    return pl.pallas_call(
        body, grid=(m // bm, n // bn),
        in_specs=[pl.BlockSpec((bm, k), lambda i, j: (i, 0)), pl.BlockSpec((k, bn), lambda i, j: (0, j)),
                  pl.BlockSpec((bm, HEAD_DIM), lambda i, j: (i, 0)),
                  pl.BlockSpec((bm, HEAD_DIM), lambda i, j: (i, 0))],
        out_specs=pl.BlockSpec((bm, bn), lambda i, j: (i, j)),
        out_shape=jax.ShapeDtypeStruct((m, n), BF16), name="qkv_rope",
        compiler_params=_params("parallel", "parallel"))(a, w, cos, sin)


def _mm_group_body(a_ref, w_ref, s_ref, o_ref):
    o_ref[...] = jnp.dot(a_ref[...], w_ref[0], preferred_element_type=F32) * s_ref[...]


def _matmul_groups(a, w, ch_scale):
    m, d = a.shape
    ng, gk, gn = w.shape
    bm = _pick(m, (768, 640, 512, 256, 128))
    return pl.pallas_call(
        _mm_group_body, grid=(m // bm, ng),
        in_specs=[pl.BlockSpec((bm, gk), lambda i, g: (i, g)), pl.BlockSpec((1, gk, gn), lambda i, g: (g, 0, 0)),
                  pl.BlockSpec((1, gn), lambda i, g: (0, g))],
        out_specs=pl.BlockSpec((bm, gn), lambda i, g: (i, g)),
        out_shape=jax.ShapeDtypeStruct((m, ng * gn), F32), name="pool_matmul",
        compiler_params=_params("parallel", "parallel"))(a, w, ch_scale.reshape(1, -1))


def _mm_gate_body(a_ref, wb_ref, wc_ref, wx_ref, b_ref, v_ref):
    a = a_ref[...]
    b_ref[...] = jnp.dot(a, wb_ref[...], preferred_element_type=F32).astype(BF16)
    v_ref[...] = (jnp.dot(a, wc_ref[...], preferred_element_type=F32)
                  * jnp.dot(a, wx_ref[...], preferred_element_type=F32)).astype(BF16)


def _matmul_gates(a, w_in):
    m, k = a.shape
    d = w_in.shape[1] // 3
    bm = _pick(m, (768, 640, 512, 256, 128))
    bn = _pick(d, (512, 256, 128))
    nj = d // bn
    a_spec = pl.BlockSpec((bm, k), lambda i, j: (i, 0))
    out = pl.BlockSpec((bm, bn), lambda i, j: (i, j))
    return pl.pallas_call(
        _mm_gate_body, grid=(m // bm, nj),
        in_specs=[a_spec,
                  pl.BlockSpec((k, bn), lambda i, j: (0, j)),
                  pl.BlockSpec((k, bn), lambda i, j: (0, j + nj)),
                  pl.BlockSpec((k, bn), lambda i, j: (0, j + 2 * nj))],
        out_specs=[out, out],
        out_shape=[jax.ShapeDtypeStruct((m, d), BF16), jax.ShapeDtypeStruct((m, d), BF16)], name="conv_in_gates",
        compiler_params=_params("parallel", "parallel"))(a, w_in, w_in, w_in)


def _mlp_body(h_ref, w1_ref, w2_ref, *rest, n_side):
    src_refs, o_ref, dst_refs = rest[:n_side], rest[n_side], rest[n_side + 1:]

    @pl.when(pl.program_id(1) == 0)
    def _():
        o_ref[...] = jnp.zeros(o_ref.shape, F32)

    u = jnp.maximum(jnp.dot(h_ref[...], w1_ref[...], preferred_element_type=F32), 0.0)
    o_ref[...] += jnp.dot((u * u).astype(BF16), w2_ref[...], preferred_element_type=F32)

    for src, dst in zip(src_refs, dst_refs):
        dst[...] = src[0].astype(BF16)


def _side_chunks(rows, n_steps):
    for chunks in range(min(n_steps, rows // HALO), 0, -1):
        if rows % (chunks * HALO) == 0:
            return chunks
    raise ValueError(f"cannot chunk {rows} rows")


def _mlp(h, w1, w2, side=()):
    m, d = h.shape
    f = w1.shape[1]
    bm = _pick(m, (768, 640, 512, 256, 128))
    bf = _pick(f, (512, 256, 128))
    n_f = f // bf
    n_steps = (m // bm) * n_f
    in_specs = [pl.BlockSpec((bm, d), lambda i, j: (i, 0), pipeline_mode=pl.Buffered(1)),
                pl.BlockSpec((d, bf), lambda i, j: (0, j)),
                pl.BlockSpec((bf, d), lambda i, j: (j, 0))]
    out_specs = [pl.BlockSpec((bm, d), lambda i, j: (i, 0))]
    out_shape = [jax.ShapeDtypeStruct((m, d), F32)]
    for w, layer in side:
        _, rows, cols = w.shape
        chunks = _side_chunks(rows, n_steps)
        chunk = lambda i, j, chunks=chunks: jnp.minimum(i * n_f + j, chunks - 1)
        in_specs.append(pl.BlockSpec((1, rows // chunks, cols),
                                     lambda i, j, layer=layer, chunk=chunk: (layer, chunk(i, j), 0)))
        out_specs.append(pl.BlockSpec((rows // chunks, cols), lambda i, j, chunk=chunk: (chunk(i, j), 0)))
        out_shape.append(jax.ShapeDtypeStruct((rows, cols), BF16))
    out = pl.pallas_call(
        functools.partial(_mlp_body, n_side=len(side)), grid=(m // bm, n_f),
        in_specs=in_specs, out_specs=out_specs, out_shape=out_shape, name="mlp",
        compiler_params=_params("arbitrary", "arbitrary"))(h, w1, w2, *[w for w, _ in side])
    return out[0], out[1:]


LOG2E = 1.4426950408889634


def _attn_body(sink_ref, bias_ref, q_ref, kp_ref, kc_ref, kn_ref, kx_ref, vp_ref, vc_ref, vn_ref, vx_ref,
               o_ref, *, kv_per_step):
    scale = HEAD_DIM ** -0.5
    bias = bias_ref[0]
    for kv in range(kv_per_step):
        cols = slice(kv * HEAD_DIM, (kv + 1) * HEAD_DIM)
        heads = [(kv * KV_GROUP + g) for g in range(KV_GROUP)]
        q4 = jnp.concatenate([q_ref[:, hd * HEAD_DIM:(hd + 1) * HEAD_DIM] for hd in heads], axis=0)
        kcat = jnp.concatenate([kp_ref[:, cols], kc_ref[:, cols], kn_ref[:, cols], kx_ref[:, cols]], axis=0)
        vcat = jnp.concatenate([vp_ref[:, cols], vc_ref[:, cols], vn_ref[:, cols], vx_ref[:, cols]], axis=0)
        s = lax.dot_general(q4, kcat, (((1,), (1,)), ((), ())), preferred_element_type=F32)
        for g, hd in enumerate(heads):
            sg = s[g * ATT_BLOCK:(g + 1) * ATT_BLOCK] + bias
            sk = sink_ref[pl.program_id(0) * kv_per_step * KV_GROUP + hd] * (1.0 / scale)
            mx = jnp.maximum(jnp.max(sg, axis=1, keepdims=True), sk)
            p = jnp.exp2((sg - mx) * (scale * LOG2E))
            den = jnp.sum(p, axis=1, keepdims=True) + jnp.exp2((sk - mx) * (scale * LOG2E))
            og = jnp.dot(p.astype(BF16), vcat, preferred_element_type=F32)
            o_ref[:, hd * HEAD_DIM:(hd + 1) * HEAD_DIM] = (og / den).astype(o_ref.dtype)


def _band_bias(n_ctx_keys):
    blk = ATT_BLOCK
    nk = 3 * blk + n_ctx_keys
    qi = jnp.arange(blk)[:, None]
    kj = jnp.arange(nk)[None, :]
    rel = kj - blk - qi
    in_window = (jnp.abs(rel) <= blk) & (kj < 3 * blk)
    is_ctx_key = kj >= 3 * blk
    variants = []
    for v in range(5):
        ok = in_window
        if v & 1:
            ok = ok & (kj >= blk)
        if v & 2:
            ok = ok & (kj < 2 * blk)
        if v == 4:
            ok = jnp.zeros_like(in_window)
        variants.append(jnp.where(ok | is_ctx_key, 0.0, -jnp.inf))
    return jnp.stack(variants).astype(F32)


def _attention(qkv, sink, *, n_x_rows, n_heads):
    t = qkv.shape[0]
    c = t - n_x_rows
    blk = ATT_BLOCK
    assert n_x_rows % c == 0 and c % blk == 0
    nkv = n_heads // KV_GROUP
    per = 2 if nkv % 2 == 0 else 1
    nbx = n_x_rows // blk
    kcol = n_heads // per
    vcol = (n_heads + nkv) // per
    clampx = lambda b: jnp.clip(b, 0, nbx - 1)

    def band(col, shift):
        return pl.BlockSpec((blk, per * HEAD_DIM), lambda kh, n: (clampx(n + shift), col + kh))

    def ctx(col):
        return pl.BlockSpec((c, per * HEAD_DIM), lambda kh, n: (n_x_rows // c, col + kh))

    def variant(kh, n):
        edge = (n == 0).astype(jnp.int32) + 2 * (n == nbx - 1).astype(jnp.int32)
        return (jnp.where(n >= nbx, 4, edge), 0, 0)

    bias = _band_bias(c)
    qcols = per * KV_GROUP * HEAD_DIM
    return pl.pallas_call(
        functools.partial(_attn_body, kv_per_step=per), grid=(nkv // per, t // blk),
        in_specs=[pl.BlockSpec(memory_space=pltpu.SMEM),
                  pl.BlockSpec((1,) + bias.shape[1:], variant),
                  pl.BlockSpec((blk, qcols), lambda kh, n: (n, kh)),
                  band(kcol, -1), band(kcol, 0), band(kcol, 1), ctx(kcol),
                  band(vcol, -1), band(vcol, 0), band(vcol, 1), ctx(vcol)],
        out_specs=pl.BlockSpec((blk, qcols), lambda kh, n: (n, kh)),
        out_shape=jax.ShapeDtypeStruct((t, n_heads * HEAD_DIM), BF16), name="attention",
        compiler_params=_params("parallel", "parallel"))(sink, bias, qkv, *([qkv] * 8))


def _rope_tables(n_x_rows, n_rows):
    tpos = jnp.arange(n_x_rows)
    n_freq = HEAD_DIM // 4
    freqs = ROPE_THETA ** (-jnp.arange(n_freq, dtype=F32) / n_freq)
    ang_r = (tpos // GRID_W).astype(F32)[:, None] * freqs
    ang_c = (tpos % GRID_W).astype(F32)[:, None] * freqs
    cos = jnp.concatenate([jnp.cos(ang_r)] * 2 + [jnp.cos(ang_c)] * 2, axis=1)
    sin = jnp.concatenate([-jnp.sin(ang_r), jnp.sin(ang_r), -jnp.sin(ang_c), jnp.sin(ang_c)], axis=1)
    pad = n_rows - n_x_rows
    cos = jnp.concatenate([cos, jnp.ones((pad, HEAD_DIM), F32)], axis=0)
    sin = jnp.concatenate([sin, jnp.zeros((pad, HEAD_DIM), F32)], axis=0)
    return cos, sin


def kernel(x, c, ctx, c_ctx, mod_w_down, mod_w_up, mod_b, ln_g, ln_b, mlp_w1, mlp_w2,
           attn_w_qkv, attn_w_o, attn_sink, pool_w, pool_scale,
           conv_w_in, conv_w, conv_b, conv_w_out):
    batch, s, d = x.shape
    assert batch == 1 and s % ROW_TILE == 0 and ctx.shape[1] % ROW_TILE == 0
    depth = mod_w_down.shape[0]
    t = s + ctx.shape[1]
    n_heads = d // HEAD_DIM
    alpha = (2 * depth) ** 0.25

    cond = jnp.zeros((8, d), F32).at[0].set(c[0]).at[1].set(c_ctx)
    cond = jnp.broadcast_to(cond, (depth, 8, d))
    mods = _vecmat(_vecmat(cond, mod_w_down, silu=True), mod_w_up, mod_b)
    mods = mods[:, :2].reshape(depth, 2, 6, d)

    cos, sin = _rope_tables(s, t)
    xs, h = _prep(x, ctx, mods)

    pool_w2d = pool_w.reshape(pool_w.shape[0], -1, pool_w.shape[-1])

    def layer_weights(i):
        kind, j = i % N_MIXERS, i // N_MIXERS
        mixer = {0: [("qkv", attn_w_qkv, j), ("wo", attn_w_o, j)],
                 1: [("pool", pool_w2d, j)],
                 2: [("cin", conv_w_in, j), ("cout", conv_w_out, j)]}[kind]
        return mixer + [("w1", mlp_w1, i), ("w2", mlp_w2, i)]

    wb = {name: w[idx].astype(BF16) for name, w, idx in layer_weights(0)}

    for i in range(depth):
        last = i == depth - 1
        kind, j = i % N_MIXERS, i // N_MIXERS
        if kind == 0:
            n_rope = (n_heads + n_heads // KV_GROUP) * HEAD_DIM
            qkv = _matmul_rope(h, wb["qkv"], cos, sin, n_rope)
            o = _attention(qkv, attn_sink[j], n_x_rows=s, n_heads=n_heads)
            y = _matmul(o, wb["wo"], F32)
        elif kind == 1:
            diffs = _pooldiff(xs, mods, layer=i, n_x_rows=s)
            y = _matmul_groups(diffs, wb["pool"].reshape(pool_w.shape[1:]), pool_scale[j])
        else:
            b, v = _matmul_gates(h, wb["cin"])
            g = _gated_conv(b, v, conv_w, conv_b, layer=j, n_x_rows=s)
            y = _matmul(g, wb["cout"], F32)
        xs, h = _ln(xs, y, mods, ln_g, ln_b, layer=i, which=0, next_layer=i, n_x_rows=s, rows=t,
                    alpha=alpha, emit_h=True)
        nxt = [] if last else layer_weights(i + 1)
        y, cast = _mlp(h, wb["w1"], wb["w2"], [(w, idx) for _, w, idx in nxt])
        wb = {name: wc for (name, _, _), wc in zip(nxt, cast)}
        need_h = (not last) and (i + 1) % N_MIXERS != 1
        xs, h = _ln(xs, y, mods, ln_g, ln_b, layer=i, which=1, next_layer=min(i + 1, depth - 1),
                    n_x_rows=s, rows=s if last else t, alpha=alpha, emit_h=need_h)
    return xs.reshape(batch, s, d)
```

```python
import functools

import jax
import jax.numpy as jnp
from jax import lax
from jax.experimental import pallas as pl
from jax.experimental.pallas import tpu as pltpu

HEAD_DIM = 128
KV_GROUP = 4
GRID_W = 64
ATT_BLOCK = 128
ROPE_THETA = 10000.0
POOL_WINDOWS = (2, 4, 8, 16)
CONV_WIDTH = 3
N_MIXERS = 3
LN_EPS = 1e-5

LANES = 128
ROW_TILE = 256
HALO = 16
VMEM_LIMIT = 56 * 1024 * 1024

F32 = jnp.float32
BF16 = jnp.bfloat16


def _pick(n, candidates):
    for c in candidates:
        if n % c == 0:
            return c
    raise ValueError(f"no tile in {candidates} divides {n}")


def _params(*sem):
    return pltpu.CompilerParams(dimension_semantics=sem, vmem_limit_bytes=VMEM_LIMIT)


def _vecmat_body(a_ref, w_ref, o_ref, *, silu):
    a = a_ref[0]
    if silu:
        a = a * jax.nn.sigmoid(a)
    o_ref[0] = jnp.dot(a.astype(BF16), w_ref[0].astype(BF16), preferred_element_type=F32)


def _vecmat_bias_body(a_ref, w_ref, b_ref, o_ref):
    o_ref[0] = jnp.dot(a_ref[0].astype(BF16), w_ref[0].astype(BF16),
                       preferred_element_type=F32) + b_ref[0]


def _vecmat(a, w, bias=None, *, silu=False):
    nl, rows, k = a.shape
    n = w.shape[2]
    bn = _pick(n, (2048, 1024, 512, 256, 128))
    in_specs = [pl.BlockSpec((1, rows, k), lambda l, j: (l, 0, 0)),
                pl.BlockSpec((1, k, bn), lambda l, j: (l, 0, j))]
    args = [a, w]
    if bias is None:
        body = functools.partial(_vecmat_body, silu=silu)
    else:
        body = _vecmat_bias_body
        in_specs.append(pl.BlockSpec((1, 1, bn), lambda l, j: (l, 0, j)))
        args.append(bias.reshape(nl, 1, n))
    return pl.pallas_call(
        body, grid=(nl, n // bn), in_specs=in_specs,
        out_specs=pl.BlockSpec((1, rows, bn), lambda l, j: (l, 0, j)),
        out_shape=jax.ShapeDtypeStruct((nl, rows, n), F32), name="adaln_vecmat",
        compiler_params=_params("parallel", "parallel"))(*args)


def _prep_body(x_ref, c_ref, m_ref, xo_ref, ho_ref, *, n_x_tiles):
    is_ctx = pl.program_id(0) >= n_x_tiles
    xv = jnp.where(is_ctx, c_ref[0], x_ref[0])
    xo_ref[...] = xv
    ho_ref[...] = (xv * (1.0 + m_ref[0, 0, 1:2, :]) + m_ref[0, 0, 0:1, :]).astype(BF16)


def _prep(x, ctx, mods):
    _, s, d = x.shape
    c = ctx.shape[1]
    t = s + c
    rt = ROW_TILE
    nx = s // rt
    return pl.pallas_call(
        functools.partial(_prep_body, n_x_tiles=nx), grid=(t // rt,),
        in_specs=[pl.BlockSpec((1, rt, d), lambda i: (0, jnp.minimum(i, nx - 1), 0)),
                  pl.BlockSpec((1, rt, d), lambda i: (0, jnp.maximum(i - nx, 0), 0)),
                  pl.BlockSpec((1, 1, 6, d), lambda i: (0, (i >= nx).astype(jnp.int32), 0, 0))],
        out_specs=[pl.BlockSpec((rt, d), lambda i: (i, 0)), pl.BlockSpec((rt, d), lambda i: (i, 0))],
        out_shape=[jax.ShapeDtypeStruct((t, d), F32), jax.ShapeDtypeStruct((t, d), BF16)], name="prep",
        compiler_params=_params("parallel"))(x, ctx, mods)


def _ln_body(x_ref, y_ref, mg_ref, mn_ref, g_ref, b_ref, xo_ref, *ho_ref, gate, shift, scale, which, alpha):
    z = alpha * x_ref[...] + mg_ref[0, 0, gate:gate + 1, :] * y_ref[...]
    mu = jnp.mean(z, axis=-1, keepdims=True)
    zc = z - mu
    var = jnp.mean(zc * zc, axis=-1, keepdims=True)
    xn = zc * lax.rsqrt(var + LN_EPS) * g_ref[0, which:which + 1, :] + b_ref[0, which:which + 1, :]
    xo_ref[...] = xn
    if ho_ref:
        ho_ref[0][...] = (xn * (1.0 + mn_ref[0, 0, scale:scale + 1, :])
                          + mn_ref[0, 0, shift:shift + 1, :]).astype(BF16)


def _ln(x, y, mods, ln_g, ln_b, *, layer, which, next_layer, n_x_rows, rows, alpha, emit_h):
    d = x.shape[1]
    rt = ROW_TILE
    nx = n_x_rows // rt
    gate = 2 + 3 * which
    shift, scale = (3, 4) if which == 0 else (0, 1)
    is_ctx = lambda i: (i >= nx).astype(jnp.int32)
    row = pl.BlockSpec((rt, d), lambda i: (i, 0))
    out_specs = [row]
    out_shape = [jax.ShapeDtypeStruct((rows, d), F32)]
    if emit_h:
        out_specs.append(row)
        out_shape.append(jax.ShapeDtypeStruct((rows, d), BF16))
    body = functools.partial(_ln_body, gate=gate, shift=shift, scale=scale, which=which, alpha=alpha)
    out = pl.pallas_call(
        body, grid=(rows // rt,),
        in_specs=[row, row,
                  pl.BlockSpec((1, 1, 6, d), lambda i: (layer, is_ctx(i), 0, 0)),
                  pl.BlockSpec((1, 1, 6, d), lambda i: (next_layer, is_ctx(i), 0, 0)),
                  pl.BlockSpec((1, 2, d), lambda i: (layer, 0, 0)),
                  pl.BlockSpec((1, 2, d), lambda i: (layer, 0, 0))],
        out_specs=out_specs, out_shape=out_shape, name="resid_layernorm",
        compiler_params=_params("parallel"))(x, y, mods, mods, ln_g, ln_b)
    return (out[0], out[1]) if emit_h else (out[0], None)


def _seq_bounds(i, rt, n_x_rows, n_rows):
    in_x = i * rt < n_x_rows
    return jnp.where(in_x, 0, n_x_rows), jnp.where(in_x, n_x_rows, n_rows)


def _pooldiff_body(xp_ref, x_ref, xn_ref, m_ref, o_ref, *, n_x_rows, n_rows, group):
    rt = x_ref.shape[0]
    i = pl.program_id(0)
    lo, hi = _seq_bounds(i, rt, n_x_rows, n_rows)
    r = i * rt + lax.broadcasted_iota(jnp.int32, (rt, 1), 0)
    xx = jnp.concatenate([xp_ref[...], x_ref[...], xn_ref[...]], axis=0)
    hh = xx * (1.0 + m_ref[0, 0, 1:2, :]) + m_ref[0, 0, 0:1, :]
    rr = i * rt - HALO + lax.broadcasted_iota(jnp.int32, (rt + 2 * HALO, 1), 0)
    hm = jnp.where((rr >= lo) & (rr < hi), hh, 0.0)
    for g, w in enumerate(POOL_WINDOWS):
        assert w & (w - 1) == 0 and w // 2 <= HALO
        run = hm[:, g * group:(g + 1) * group]
        width = 1
        while width < w:
            run = run[:run.shape[0] - width] + run[width:]
            width *= 2
        first = HALO - w // 2
        cnt = (jnp.minimum(r + (w - w // 2), hi) - jnp.maximum(r - w // 2, lo)).astype(F32)
        o_ref[:, g * group:(g + 1) * group] = (
            run[first:first + rt] * (1.0 / cnt) - hh[HALO:HALO + rt, g * group:(g + 1) * group]).astype(BF16)


def _halo_specs(rt, d, n_rows):
    per = rt // HALO
    last = n_rows // HALO - 1
    return (pl.BlockSpec((HALO, d), lambda i: (jnp.maximum(i * per - 1, 0), 0)),
            pl.BlockSpec((rt, d), lambda i: (i, 0)),
            pl.BlockSpec((HALO, d), lambda i: (jnp.minimum((i + 1) * per, last), 0)))


def _pooldiff(x, mods, *, layer, n_x_rows):
    t, d = x.shape
    rt = ROW_TILE
    nx = n_x_rows // rt
    prev, cur, nxt = _halo_specs(rt, d, t)
    body = functools.partial(_pooldiff_body, n_x_rows=n_x_rows, n_rows=t, group=d // len(POOL_WINDOWS))
    return pl.pallas_call(
        body, grid=(t // rt,),
        in_specs=[prev, cur, nxt,
                  pl.BlockSpec((1, 1, 6, d), lambda i: (layer, (i >= nx).astype(jnp.int32), 0, 0))],
        out_specs=pl.BlockSpec((rt, d), lambda i: (i, 0)),
        out_shape=jax.ShapeDtypeStruct((t, d), BF16), name="pool_diff",
        compiler_params=_params("parallel"))(x, x, x, mods)


def _conv_body(b_ref, vp_ref, v_ref, vn_ref, w_ref, cb_ref, o_ref, *, n_x_rows, n_rows):
    rt = v_ref.shape[0]
    i = pl.program_id(0)
    lo, hi = _seq_bounds(i, rt, n_x_rows, n_rows)
    r = i * rt + lax.broadcasted_iota(jnp.int32, (rt, 1), 0)
    vv = jnp.concatenate([vp_ref[...], v_ref[...], vn_ref[...]], axis=0).astype(F32)
    pad = (CONV_WIDTH - 1) // 2
    acc = jnp.zeros(v_ref.shape, F32)
    for k in range(CONV_WIDTH):
        off = k - pad
        ok = (r + off >= lo) & (r + off < hi)
        acc = acc + jnp.where(ok, vv[HALO + off:HALO + off + rt], 0.0) * w_ref[0, k:k + 1, :]
    o_ref[...] = (b_ref[...].astype(F32) * (acc + cb_ref[0])).astype(BF16)


def _gated_conv(b, v, conv_w, conv_b, *, layer, n_x_rows):
    t, d = v.shape
    rt = ROW_TILE
    prev, cur, nxt = _halo_specs(rt, d, t)
    body = functools.partial(_conv_body, n_x_rows=n_x_rows, n_rows=t)
    return pl.pallas_call(
        body, grid=(t // rt,),
        in_specs=[cur, prev, cur, nxt,
                  pl.BlockSpec((1, CONV_WIDTH, d), lambda i: (layer, 0, 0)),
                  pl.BlockSpec((1, 1, d), lambda i: (layer, 0, 0))],
        out_specs=pl.BlockSpec((rt, d), lambda i: (i, 0)),
        out_shape=jax.ShapeDtypeStruct((t, d), BF16), name="gated_conv",
        compiler_params=_params("parallel"))(b, v, v, v, conv_w, conv_b.reshape(-1, 1, d))


ROW_TILES = (1024, 768, 640, 512, 256, 128)


def _mm_tiles(m, n):
    return _pick(m, ROW_TILES), _pick(n, (1024, 512, 256, 128))


def _side_chunks(rows, n_steps):
    for chunks in range(min(n_steps, rows // HALO), 0, -1):
        if rows % (chunks * HALO) == 0:
            return chunks
    raise ValueError(f"cannot chunk {rows} rows")


def _side_casts(side, n_steps, step_of):
    in_specs, out_specs, out_shape = [], [], []
    for w, layer in side:
        _, rows, cols = w.shape
        chunks = _side_chunks(rows, n_steps)
        chunk = lambda *ids, chunks=chunks: jnp.minimum(step_of(*ids), chunks - 1)
        in_specs.append(pl.BlockSpec((1, rows // chunks, cols),
                                     lambda *ids, layer=layer, chunk=chunk: (layer, chunk(*ids), 0)))
        out_specs.append(pl.BlockSpec((rows // chunks, cols), lambda *ids, chunk=chunk: (chunk(*ids), 0)))
        out_shape.append(jax.ShapeDtypeStruct((rows, cols), BF16))
    return in_specs, out_specs, out_shape, [w for w, _ in side]


def _cast_side(src_refs, dst_refs):
    for src, dst in zip(src_refs, dst_refs):
        dst[...] = src[0].astype(BF16)


def _mm_body(a_ref, w_ref, o_ref):
    o_ref[...] = jnp.dot(a_ref[...], w_ref[...], preferred_element_type=F32).astype(o_ref.dtype)


def _matmul(a, w, out_dtype):
    m, k = a.shape
    n = w.shape[1]
    bm, bn = _mm_tiles(m, n)
    return pl.pallas_call(
        _mm_body, grid=(m // bm, n // bn),
        in_specs=[pl.BlockSpec((bm, k), lambda i, j: (i, 0)), pl.BlockSpec((k, bn), lambda i, j: (0, j))],
        out_specs=pl.BlockSpec((bm, bn), lambda i, j: (i, j)),
        out_shape=jax.ShapeDtypeStruct((m, n), out_dtype), name="matmul",
        compiler_params=_params("parallel", "parallel"))(a, w)


def _mm_rope_body(a_ref, w_ref, c_ref, s_ref, o_ref, *, n_rope_tiles):
    rotated = pl.program_id(1) < n_rope_tiles
    acc = jnp.dot(a_ref[...], w_ref[...], preferred_element_type=F32)
    bn = acc.shape[1]
    lane = lax.broadcasted_iota(jnp.int32, acc.shape, 1)
    quarter = HEAD_DIM // 4
    partner = jnp.where((lane & quarter) == 0,
                        pltpu.roll(acc, bn - quarter, 1), pltpu.roll(acc, quarter, 1))
    reps = bn // HEAD_DIM
    cos = jnp.concatenate([jnp.where(rotated, c_ref[...], 1.0)] * reps, axis=1)
    sin = jnp.concatenate([jnp.where(rotated, s_ref[...], 0.0)] * reps, axis=1)
    o_ref[...] = (acc * cos + partner * sin).astype(o_ref.dtype)


def _matmul_rope(a, w, cos, sin, n_rope_cols):
    m, k = a.shape
    n = w.shape[1]
    bm, _ = _mm_tiles(m, n)
    bn = _pick(n - n_rope_cols, (1024, 512, 256, 128))
    assert n_rope_cols % bn == 0
    body = functools.partial(_mm_rope_body, n_rope_tiles=n_rope_cols // bn)
    return pl.pallas_call(
        body, grid=(m // bm, n // bn),
        in_specs=[pl.BlockSpec((bm, k), lambda i, j: (i, 0)), pl.BlockSpec((k, bn), lambda i, j: (0, j)),
                  pl.BlockSpec((bm, HEAD_DIM), lambda i, j: (i, 0)),
                  pl.BlockSpec((bm, HEAD_DIM), lambda i, j: (i, 0))],
        out_specs=pl.BlockSpec((bm, bn), lambda i, j: (i, j)),
        out_shape=jax.ShapeDtypeStruct((m, n), BF16), name="qkv_rope",
        compiler_params=_params("parallel", "parallel"))(a, w, cos, sin)


def _mm_group_body(a_ref, w_ref, s_ref, o_ref):
    o_ref[...] = jnp.dot(a_ref[...], w_ref[0], preferred_element_type=F32) * s_ref[...]


def _matmul_groups(a, w, ch_scale):
    m, d = a.shape
    ng, gk, gn = w.shape
    bm = _pick(m, (768, 640, 512, 256, 128))
    return pl.pallas_call(
        _mm_group_body, grid=(m // bm, ng),
        in_specs=[pl.BlockSpec((bm, gk), lambda i, g: (i, g)), pl.BlockSpec((1, gk, gn), lambda i, g: (g, 0, 0)),
                  pl.BlockSpec((1, gn), lambda i, g: (0, g))],
        out_specs=pl.BlockSpec((bm, gn), lambda i, g: (i, g)),
        out_shape=jax.ShapeDtypeStruct((m, ng * gn), F32), name="pool_matmul",
        compiler_params=_params("parallel", "parallel"))(a, w, ch_scale.reshape(1, -1))


def _mm_gate_body(a_ref, wb_ref, wc_ref, wx_ref, b_ref, v_ref):
    a = a_ref[...]
    b_ref[...] = jnp.dot(a, wb_ref[...], preferred_element_type=F32).astype(BF16)
    v_ref[...] = (jnp.dot(a, wc_ref[...], preferred_element_type=F32)
                  * jnp.dot(a, wx_ref[...], preferred_element_type=F32)).astype(BF16)


def _matmul_gates(a, w_in):
    m, k = a.shape
    d = w_in.shape[1] // 3
    bm = _pick(m, (768, 640, 512, 256, 128))
    bn = _pick(d, (512, 256, 128))
    nj = d // bn
    a_spec = pl.BlockSpec((bm, k), lambda i, j: (i, 0))
    out = pl.BlockSpec((bm, bn), lambda i, j: (i, j))
    return pl.pallas_call(
        _mm_gate_body, grid=(m // bm, nj),
        in_specs=[a_spec,
                  pl.BlockSpec((k, bn), lambda i, j: (0, j)),
                  pl.BlockSpec((k, bn), lambda i, j: (0, j + nj)),
                  pl.BlockSpec((k, bn), lambda i, j: (0, j + 2 * nj))],
        out_specs=[out, out],
        out_shape=[jax.ShapeDtypeStruct((m, d), BF16), jax.ShapeDtypeStruct((m, d), BF16)], name="conv_in_gates",
        compiler_params=_params("parallel", "parallel"))(a, w_in, w_in, w_in)


def _mlp_body(h_ref, w1_ref, w2_ref, *rest, n_side):
    src_refs, o_ref, dst_refs = rest[:n_side], rest[n_side], rest[n_side + 1:]

    @pl.when(pl.program_id(1) == 0)
    def _():
        o_ref[...] = jnp.zeros(o_ref.shape, F32)

    u = jnp.maximum(jnp.dot(h_ref[...], w1_ref[...], preferred_element_type=F32), 0.0)
    o_ref[...] += jnp.dot((u * u).astype(BF16), w2_ref[...], preferred_element_type=F32)

    _cast_side(src_refs, dst_refs)


def _mlp(h, w1, w2, side=()):
    m, d = h.shape
    f = w1.shape[1]
    bm = _pick(m, ROW_TILES)
    bf = _pick(f, (512, 256, 128))
    n_f = f // bf
    out_mode = {} if 2 * bm * d * 4 <= VMEM_LIMIT // 2 else {"pipeline_mode": pl.Buffered(1)}
    side_in, side_out, side_shape, side_args = _side_casts(
        side, (m // bm) * n_f, lambda i, j: i * n_f + j)
    out = pl.pallas_call(
        functools.partial(_mlp_body, n_side=len(side)), grid=(m // bm, n_f),
        in_specs=[pl.BlockSpec((bm, d), lambda i, j: (i, 0), pipeline_mode=pl.Buffered(1)),
                  pl.BlockSpec((d, bf), lambda i, j: (0, j)),
                  pl.BlockSpec((bf, d), lambda i, j: (j, 0))] + side_in,
        out_specs=[pl.BlockSpec((bm, d), lambda i, j: (i, 0), **out_mode)] + side_out,
        out_shape=[jax.ShapeDtypeStruct((m, d), F32)] + side_shape, name="mlp",
        compiler_params=_params("arbitrary", "arbitrary"))(h, w1, w2, *side_args)
    return out[0], out[1:]


LOG2E = 1.4426950408889634


def _attn_body(sink_ref, bias_ref, q_ref, kp_ref, kc_ref, kn_ref, kx_ref, vp_ref, vc_ref, vn_ref, vx_ref,
               *rest, kv_per_step, n_side):
    src_refs, o_ref, dst_refs = rest[:n_side], rest[n_side], rest[n_side + 1:]
    _attn_heads(sink_ref, bias_ref, q_ref, kp_ref, kc_ref, kn_ref, kx_ref, vp_ref, vc_ref, vn_ref, vx_ref,
                o_ref, kv_per_step=kv_per_step)
    _cast_side(src_refs, dst_refs)


def _attn_heads(sink_ref, bias_ref, q_ref, kp_ref, kc_ref, kn_ref, kx_ref, vp_ref, vc_ref, vn_ref, vx_ref,
                o_ref, *, kv_per_step):
    scale = HEAD_DIM ** -0.5
    bias = bias_ref[0]
    for kv in range(kv_per_step):
        cols = slice(kv * HEAD_DIM, (kv + 1) * HEAD_DIM)
        heads = [(kv * KV_GROUP + g) for g in range(KV_GROUP)]
        q4 = jnp.concatenate([q_ref[:, hd * HEAD_DIM:(hd + 1) * HEAD_DIM] for hd in heads], axis=0)
        kcat = jnp.concatenate([kp_ref[:, cols], kc_ref[:, cols], kn_ref[:, cols], kx_ref[:, cols]], axis=0)
        vcat = jnp.concatenate([vp_ref[:, cols], vc_ref[:, cols], vn_ref[:, cols], vx_ref[:, cols]], axis=0)
        s = lax.dot_general(q4, kcat, (((1,), (1,)), ((), ())), preferred_element_type=F32)
        for g, hd in enumerate(heads):
            sg = s[g * ATT_BLOCK:(g + 1) * ATT_BLOCK] + bias
            sk = sink_ref[pl.program_id(0) * kv_per_step * KV_GROUP + hd] * (1.0 / scale)
            mx = jnp.maximum(jnp.max(sg, axis=1, keepdims=True), sk)
            p = jnp.exp2((sg - mx) * (scale * LOG2E))
            den = jnp.sum(p, axis=1, keepdims=True) + jnp.exp2((sk - mx) * (scale * LOG2E))
            og = jnp.dot(p.astype(BF16), vcat, preferred_element_type=F32)
            o_ref[:, hd * HEAD_DIM:(hd + 1) * HEAD_DIM] = (og / den).astype(o_ref.dtype)


def _band_bias(n_ctx_keys):
    blk = ATT_BLOCK
    nk = 3 * blk + n_ctx_keys
    qi = jnp.arange(blk)[:, None]
    kj = jnp.arange(nk)[None, :]
    rel = kj - blk - qi
    in_window = (jnp.abs(rel) <= blk) & (kj < 3 * blk)
    is_ctx_key = kj >= 3 * blk
    variants = []
    for v in range(5):
        ok = in_window
        if v & 1:
            ok = ok & (kj >= blk)
        if v & 2:
            ok = ok & (kj < 2 * blk)
        if v == 4:
            ok = jnp.zeros_like(in_window)
        variants.append(jnp.where(ok | is_ctx_key, 0.0, -jnp.inf))
    return jnp.stack(variants).astype(F32)


def _attention(qkv, sink, *, n_x_rows, n_heads, q_rows, side=()):
    t = qkv.shape[0]
    c = t - n_x_rows
    blk = ATT_BLOCK
    assert n_x_rows % c == 0 and c % blk == 0
    nkv = n_heads // KV_GROUP
    per = _pick(nkv, (4, 2, 1))
    nbx = n_x_rows // blk
    kcol = n_heads // per
    vcol = (n_heads + nkv) // per
    clampx = lambda b: jnp.clip(b, 0, nbx - 1)

    def band(col, shift):
        return pl.BlockSpec((blk, per * HEAD_DIM), lambda kh, n: (clampx(n + shift), col + kh))

    def ctx(col):
        return pl.BlockSpec((c, per * HEAD_DIM), lambda kh, n: (n_x_rows // c, col + kh))

    def variant(kh, n):
        edge = (n == 0).astype(jnp.int32) + 2 * (n == nbx - 1).astype(jnp.int32)
        return (jnp.where(n >= nbx, 4, edge), 0, 0)

    bias = _band_bias(c)
    qcols = per * KV_GROUP * HEAD_DIM
    n_q = q_rows // blk
    side_in, side_out, side_shape, side_args = _side_casts(
        side, (nkv // per) * n_q, lambda kh, n: kh * n_q + n)
    out = pl.pallas_call(
        functools.partial(_attn_body, kv_per_step=per, n_side=len(side)), grid=(nkv // per, n_q),
        in_specs=[pl.BlockSpec(memory_space=pltpu.SMEM),
                  pl.BlockSpec((1,) + bias.shape[1:], variant),
                  pl.BlockSpec((blk, qcols), lambda kh, n: (n, kh)),
                  band(kcol, -1), band(kcol, 0), band(kcol, 1), ctx(kcol),
                  band(vcol, -1), band(vcol, 0), band(vcol, 1), ctx(vcol)] + side_in,
        out_specs=[pl.BlockSpec((blk, qcols), lambda kh, n: (n, kh))] + side_out,
        out_shape=[jax.ShapeDtypeStruct((q_rows, n_heads * HEAD_DIM), BF16)] + side_shape, name="attention",
        compiler_params=_params("arbitrary", "arbitrary"))(sink, bias, qkv, *([qkv] * 8), *side_args)
    return out[0], out[1:]


def _rope_tables(n_x_rows, n_rows):
    tpos = jnp.arange(n_x_rows)
    n_freq = HEAD_DIM // 4
    freqs = ROPE_THETA ** (-jnp.arange(n_freq, dtype=F32) / n_freq)
    ang_r = (tpos // GRID_W).astype(F32)[:, None] * freqs
    ang_c = (tpos % GRID_W).astype(F32)[:, None] * freqs
    cos = jnp.concatenate([jnp.cos(ang_r)] * 2 + [jnp.cos(ang_c)] * 2, axis=1)
    sin = jnp.concatenate([-jnp.sin(ang_r), jnp.sin(ang_r), -jnp.sin(ang_c), jnp.sin(ang_c)], axis=1)
    pad = n_rows - n_x_rows
    cos = jnp.concatenate([cos, jnp.ones((pad, HEAD_DIM), F32)], axis=0)
    sin = jnp.concatenate([sin, jnp.zeros((pad, HEAD_DIM), F32)], axis=0)
    return cos, sin


def kernel(x, c, ctx, c_ctx, mod_w_down, mod_w_up, mod_b, ln_g, ln_b, mlp_w1, mlp_w2,
           attn_w_qkv, attn_w_o, attn_sink, pool_w, pool_scale,
           conv_w_in, conv_w, conv_b, conv_w_out):
    batch, s, d = x.shape
    assert batch == 1 and s % ROW_TILE == 0 and ctx.shape[1] % ROW_TILE == 0
    depth = mod_w_down.shape[0]
    t = s + ctx.shape[1]
    n_heads = d // HEAD_DIM
    alpha = (2 * depth) ** 0.25

    cond = jnp.zeros((8, d), F32).at[0].set(c[0]).at[1].set(c_ctx)
    cond = jnp.broadcast_to(cond, (depth, 8, d))
    mods = _vecmat(_vecmat(cond, mod_w_down, silu=True), mod_w_up, mod_b)
    mods = mods[:, :2].reshape(depth, 2, 6, d)

    cos, sin = _rope_tables(s, t)
    xs, h = _prep(x, ctx, mods)

    pool_w2d = pool_w.reshape(pool_w.shape[0], -1, pool_w.shape[-1])

    def layer_weights(i):
        kind, j = i % N_MIXERS, i // N_MIXERS
        mixer = {0: [("qkv", attn_w_qkv, j), ("wo", attn_w_o, j)],
                 1: [("pool", pool_w2d, j)],
                 2: [("cin", conv_w_in, j), ("cout", conv_w_out, j)]}[kind]
        return mixer + [("w1", mlp_w1, i), ("w2", mlp_w2, i)]

    wb = {name: w[idx].astype(BF16) for name, w, idx in layer_weights(0)[:-2]}

    for i in range(depth):
        last = i == depth - 1
        rows = s if last else t
        kind, j = i % N_MIXERS, i // N_MIXERS
        if kind == 0:
            n_rope = (n_heads + n_heads // KV_GROUP) * HEAD_DIM
            qkv = _matmul_rope(h, wb["qkv"], cos, sin, n_rope)
            own_mlp = [] if "w1" in wb else layer_weights(i)[-2:]
            o, cast = _attention(qkv, attn_sink[j], n_x_rows=s, n_heads=n_heads, q_rows=rows,
                                 side=[(w, idx) for _, w, idx in own_mlp])
            wb.update({name: wc for (name, _, _), wc in zip(own_mlp, cast)})
            y = _matmul(o, wb["wo"], F32)
        elif kind == 1:
            diffs = _pooldiff(xs, mods, layer=i, n_x_rows=s)
            y = _matmul_groups(diffs, wb["pool"].reshape(pool_w.shape[1:]), pool_scale[j])
        else:
            b, v = _matmul_gates(h, wb["cin"])
            g = _gated_conv(b, v, conv_w, conv_b, layer=j, n_x_rows=s)
            y = _matmul(g, wb["cout"], F32)
        xs, h = _ln(xs, y, mods, ln_g, ln_b, layer=i, which=0, next_layer=i, n_x_rows=s, rows=rows,
                    alpha=alpha, emit_h=True)
        for name, w, idx in layer_weights(i)[-2:]:
            if name not in wb:
                wb[name] = w[idx].astype(BF16)
        nxt = [] if last else layer_weights(i + 1)
        y, cast = _mlp(h, wb["w1"], wb["w2"], [(w, idx) for _, w, idx in nxt])
        wb = {name: wc for (name, _, _), wc in zip(nxt, cast)}
        need_h = (not last) and (i + 1) % N_MIXERS != 1
        xs, h = _ln(xs, y, mods, ln_g, ln_b, layer=i, which=1, next_layer=min(i + 1, depth - 1),
                    n_x_rows=s, rows=rows, alpha=alpha, emit_h=need_h)
    return xs.reshape(batch, s, d)
```

```python
import functools

import jax
import jax.numpy as jnp
from jax import lax
from jax.experimental import pallas as pl
from jax.experimental.pallas import tpu as pltpu

HEAD_DIM = 128
KV_GROUP = 4
GRID_W = 64
ATT_BLOCK = 128
ROPE_THETA = 10000.0
POOL_WINDOWS = (2, 4, 8, 16)
CONV_WIDTH = 3
N_MIXERS = 3
LN_EPS = 1e-5

LANES = 128
ROW_TILE = 256
HALO = 16
VMEM_LIMIT = 56 * 1024 * 1024

F32 = jnp.float32
BF16 = jnp.bfloat16


def _pick(n, candidates):
    for c in candidates:
        if n % c == 0:
            return c
    raise ValueError(f"no tile in {candidates} divides {n}")


def _params(*sem):
    return pltpu.CompilerParams(dimension_semantics=sem, vmem_limit_bytes=VMEM_LIMIT)


def _vecmat_body(a_ref, w_ref, o_ref, *, silu):
    a = a_ref[0]
    if silu:
        a = a * jax.nn.sigmoid(a)
    o_ref[0] = jnp.dot(a.astype(BF16), w_ref[0].astype(BF16), preferred_element_type=F32)


def _vecmat_bias_body(a_ref, w_ref, b_ref, o_ref):
    o_ref[0] = jnp.dot(a_ref[0].astype(BF16), w_ref[0].astype(BF16),
                       preferred_element_type=F32) + b_ref[0]


def _vecmat(a, w, bias=None, *, silu=False):
    nl, rows, k = a.shape
    n = w.shape[2]
    bn = _pick(n, (2048, 1024, 512, 256, 128))
    in_specs = [pl.BlockSpec((1, rows, k), lambda l, j: (l, 0, 0)),
                pl.BlockSpec((1, k, bn), lambda l, j: (l, 0, j))]
    args = [a, w]
    if bias is None:
        body = functools.partial(_vecmat_body, silu=silu)
    else:
        body = _vecmat_bias_body
        in_specs.append(pl.BlockSpec((1, 1, bn), lambda l, j: (l, 0, j)))
        args.append(bias.reshape(nl, 1, n))
    return pl.pallas_call(
        body, grid=(nl, n // bn), in_specs=in_specs,
        out_specs=pl.BlockSpec((1, rows, bn), lambda l, j: (l, 0, j)),
        out_shape=jax.ShapeDtypeStruct((nl, rows, n), F32), name="adaln_vecmat",
        compiler_params=_params("parallel", "parallel"))(*args)


def _prep_body(x_ref, c_ref, m_ref, xo_ref, ho_ref, *, n_x_tiles):
    is_ctx = pl.program_id(0) >= n_x_tiles
    xv = jnp.where(is_ctx, c_ref[0], x_ref[0])
    xo_ref[...] = xv
    ho_ref[...] = (xv * (1.0 + m_ref[0, 0, 1:2, :]) + m_ref[0, 0, 0:1, :]).astype(BF16)


def _prep(x, ctx, mods):
    _, s, d = x.shape
    c = ctx.shape[1]
    t = s + c
    rt = ROW_TILE
    nx = s // rt
    return pl.pallas_call(
        functools.partial(_prep_body, n_x_tiles=nx), grid=(t // rt,),
        in_specs=[pl.BlockSpec((1, rt, d), lambda i: (0, jnp.minimum(i, nx - 1), 0)),
                  pl.BlockSpec((1, rt, d), lambda i: (0, jnp.maximum(i - nx, 0), 0)),
                  pl.BlockSpec((1, 1, 6, d), lambda i: (0, (i >= nx).astype(jnp.int32), 0, 0))],
        out_specs=[pl.BlockSpec((rt, d), lambda i: (i, 0)), pl.BlockSpec((rt, d), lambda i: (i, 0))],
        out_shape=[jax.ShapeDtypeStruct((t, d), F32), jax.ShapeDtypeStruct((t, d), BF16)], name="prep",
        compiler_params=_params("parallel"))(x, ctx, mods)


def _ln_body(*refs, gate, shift, scale, which, alpha, mixed):
    if mixed:
        z_ref, mn_ref, g_ref, b_ref, xo_ref, *ho_ref = refs
        z = z_ref[...]
    else:
        x_ref, y_ref, mg_ref, mn_ref, g_ref, b_ref, xo_ref, *ho_ref = refs
        z = alpha * x_ref[...] + mg_ref[0, 0, gate:gate + 1, :] * y_ref[...]
    mu = jnp.mean(z, axis=-1, keepdims=True)
    zc = z - mu
    var = jnp.mean(zc * zc, axis=-1, keepdims=True)
    xn = zc * lax.rsqrt(var + LN_EPS) * g_ref[0, which:which + 1, :] + b_ref[0, which:which + 1, :]
    xo_ref[...] = xn
    if ho_ref:
        ho_ref[0][...] = (xn * (1.0 + mn_ref[0, 0, scale:scale + 1, :])
                          + mn_ref[0, 0, shift:shift + 1, :]).astype(BF16)


def _ln(x, y, mods, ln_g, ln_b, *, layer, which, next_layer, n_x_rows, rows, alpha, emit_h):
    d = x.shape[1]
    rt = ROW_TILE
    nx = n_x_rows // rt
    gate = 2 + 3 * which
    shift, scale = (3, 4) if which == 0 else (0, 1)
    is_ctx = lambda i: (i >= nx).astype(jnp.int32)
    row = pl.BlockSpec((rt, d), lambda i: (i, 0))
    out_specs = [row]
    out_shape = [jax.ShapeDtypeStruct((rows, d), F32)]
    if emit_h:
        out_specs.append(row)
        out_shape.append(jax.ShapeDtypeStruct((rows, d), BF16))
    mixed = y is None
    body = functools.partial(_ln_body, gate=gate, shift=shift, scale=scale, which=which, alpha=alpha,
                             mixed=mixed)
    mod_spec = lambda l: pl.BlockSpec((1, 1, 6, d), lambda i: (l, is_ctx(i), 0, 0))
    ln_spec = pl.BlockSpec((1, 2, d), lambda i: (layer, 0, 0))
    if mixed:
        in_specs, args = [row, mod_spec(next_layer), ln_spec, ln_spec], (x, mods, ln_g, ln_b)
    else:
        in_specs = [row, row, mod_spec(layer), mod_spec(next_layer), ln_spec, ln_spec]
        args = (x, y, mods, mods, ln_g, ln_b)
    out = pl.pallas_call(
        body, grid=(rows // rt,), in_specs=in_specs,
        out_specs=out_specs, out_shape=out_shape, name="resid_layernorm",
        compiler_params=_params("parallel"))(*args)
    return (out[0], out[1]) if emit_h else (out[0], None)


def _seq_bounds(i, rt, n_x_rows, n_rows):
    in_x = i * rt < n_x_rows
    return jnp.where(in_x, 0, n_x_rows), jnp.where(in_x, n_x_rows, n_rows)


def _pooldiff_body(xp_ref, x_ref, xn_ref, m_ref, o_ref, *, n_x_rows, n_rows, group):
    rt = x_ref.shape[0]
    i = pl.program_id(0)
    lo, hi = _seq_bounds(i, rt, n_x_rows, n_rows)
    r = i * rt + lax.broadcasted_iota(jnp.int32, (rt, 1), 0)
    xx = jnp.concatenate([xp_ref[...], x_ref[...], xn_ref[...]], axis=0)
    hh = xx * (1.0 + m_ref[0, 0, 1:2, :]) + m_ref[0, 0, 0:1, :]
    rr = i * rt - HALO + lax.broadcasted_iota(jnp.int32, (rt + 2 * HALO, 1), 0)
    hm = jnp.where((rr >= lo) & (rr < hi), hh, 0.0)
    for g, w in enumerate(POOL_WINDOWS):
        assert w & (w - 1) == 0 and w // 2 <= HALO
        run = hm[:, g * group:(g + 1) * group]
        width = 1
        while width < w:
            run = run[:run.shape[0] - width] + run[width:]
            width *= 2
        first = HALO - w // 2
        cnt = (jnp.minimum(r + (w - w // 2), hi) - jnp.maximum(r - w // 2, lo)).astype(F32)
        o_ref[:, g * group:(g + 1) * group] = (
            run[first:first + rt] * (1.0 / cnt) - hh[HALO:HALO + rt, g * group:(g + 1) * group]).astype(BF16)


def _halo_specs(rt, d, n_rows):
    per = rt // HALO
    last = n_rows // HALO - 1
    return (pl.BlockSpec((HALO, d), lambda i: (jnp.maximum(i * per - 1, 0), 0)),
            pl.BlockSpec((rt, d), lambda i: (i, 0)),
            pl.BlockSpec((HALO, d), lambda i: (jnp.minimum((i + 1) * per, last), 0)))


def _pooldiff(x, mods, *, layer, n_x_rows):
    t, d = x.shape
    rt = ROW_TILE
    nx = n_x_rows // rt
    prev, cur, nxt = _halo_specs(rt, d, t)
    body = functools.partial(_pooldiff_body, n_x_rows=n_x_rows, n_rows=t, group=d // len(POOL_WINDOWS))
    return pl.pallas_call(
        body, grid=(t // rt,),
        in_specs=[prev, cur, nxt,
                  pl.BlockSpec((1, 1, 6, d), lambda i: (layer, (i >= nx).astype(jnp.int32), 0, 0))],
        out_specs=pl.BlockSpec((rt, d), lambda i: (i, 0)),
        out_shape=jax.ShapeDtypeStruct((t, d), BF16), name="pool_diff",
        compiler_params=_params("parallel"))(x, x, x, mods)


def _conv_body(b_ref, vp_ref, v_ref, vn_ref, w_ref, cb_ref, o_ref, *, n_x_rows, n_rows):
    rt = v_ref.shape[0]
    i = pl.program_id(0)
    lo, hi = _seq_bounds(i, rt, n_x_rows, n_rows)
    r = i * rt + lax.broadcasted_iota(jnp.int32, (rt, 1), 0)
    vv = jnp.concatenate([vp_ref[...], v_ref[...], vn_ref[...]], axis=0).astype(F32)
    pad = (CONV_WIDTH - 1) // 2
    acc = jnp.zeros(v_ref.shape, F32)
    for k in range(CONV_WIDTH):
        off = k - pad
        ok = (r + off >= lo) & (r + off < hi)
        acc = acc + jnp.where(ok, vv[HALO + off:HALO + off + rt], 0.0) * w_ref[0, k:k + 1, :]
    o_ref[...] = (b_ref[...].astype(F32) * (acc + cb_ref[0])).astype(BF16)


def _gated_conv(b, v, conv_w, conv_b, *, layer, n_x_rows):
    t, d = v.shape
    rt = ROW_TILE
    prev, cur, nxt = _halo_specs(rt, d, t)
    body = functools.partial(_conv_body, n_x_rows=n_x_rows, n_rows=t)
    return pl.pallas_call(
        body, grid=(t // rt,),
        in_specs=[cur, prev, cur, nxt,
                  pl.BlockSpec((1, CONV_WIDTH, d), lambda i: (layer, 0, 0)),
                  pl.BlockSpec((1, 1, d), lambda i: (layer, 0, 0))],
        out_specs=pl.BlockSpec((rt, d), lambda i: (i, 0)),
        out_shape=jax.ShapeDtypeStruct((t, d), BF16), name="gated_conv",
        compiler_params=_params("parallel"))(b, v, v, v, conv_w, conv_b.reshape(-1, 1, d))


ROW_TILES = (1024, 768, 640, 512, 256, 128)


def _mm_tiles(m, n):
    return _pick(m, ROW_TILES), _pick(n, (1024, 512, 256, 128))


def _side_chunks(rows, n_steps):
    for chunks in range(min(n_steps, rows // HALO), 0, -1):
        if rows % (chunks * HALO) == 0:
            return chunks
    raise ValueError(f"cannot chunk {rows} rows")


def _side_casts(side, n_steps, step_of):
    in_specs, out_specs, out_shape = [], [], []
    for w, layer in side:
        _, rows, cols = w.shape
        chunks = _side_chunks(rows, n_steps)
        chunk = lambda *ids, chunks=chunks: jnp.minimum(step_of(*ids), chunks - 1)
        in_specs.append(pl.BlockSpec((1, rows // chunks, cols),
                                     lambda *ids, layer=layer, chunk=chunk: (layer, chunk(*ids), 0)))
        out_specs.append(pl.BlockSpec((rows // chunks, cols), lambda *ids, chunk=chunk: (chunk(*ids), 0)))
        out_shape.append(jax.ShapeDtypeStruct((rows, cols), BF16))
    return in_specs, out_specs, out_shape, [w for w, _ in side]


def _cast_side(src_refs, dst_refs):
    for src, dst in zip(src_refs, dst_refs):
        dst[...] = src[0].astype(BF16)


def _mm_rope_body(a_ref, w_ref, c_ref, s_ref, o_ref, *, n_rope_tiles):
    rotated = pl.program_id(1) < n_rope_tiles
    bm, bn = o_ref.shape
    quarter = HEAD_DIM // 4
    reps = bn // HEAD_DIM
    for rows in (slice(0, bm // 2), slice(bm // 2, bm)):
        acc = jnp.dot(a_ref[rows, :], w_ref[...], preferred_element_type=F32)
        lane = lax.broadcasted_iota(jnp.int32, acc.shape, 1)
        partner = jnp.where((lane & quarter) == 0,
                            pltpu.roll(acc, bn - quarter, 1), pltpu.roll(acc, quarter, 1))
        cos = jnp.concatenate([jnp.where(rotated, c_ref[rows, :], 1.0)] * reps, axis=1)
        sin = jnp.concatenate([jnp.where(rotated, s_ref[rows, :], 0.0)] * reps, axis=1)
        o_ref[rows, :] = (acc * cos + partner * sin).astype(o_ref.dtype)


def _matmul_rope(a, w, cos, sin, n_rope_cols):
    m, k = a.shape
    n = w.shape[1]
    bm, _ = _mm_tiles(m, n)
    bn = _pick(n - n_rope_cols, (1024, 512, 256, 128))
    assert n_rope_cols % bn == 0
    body = functools.partial(_mm_rope_body, n_rope_tiles=n_rope_cols // bn)
    return pl.pallas_call(
        body, grid=(m // bm, n // bn),
        in_specs=[pl.BlockSpec((bm, k), lambda i, j: (i, 0)), pl.BlockSpec((k, bn), lambda i, j: (0, j)),
                  pl.BlockSpec((bm, HEAD_DIM), lambda i, j: (i, 0)),
                  pl.BlockSpec((bm, HEAD_DIM), lambda i, j: (i, 0))],
        out_specs=pl.BlockSpec((bm, bn), lambda i, j: (i, j)),
        out_shape=jax.ShapeDtypeStruct((m, n), BF16), name="qkv_rope",
        compiler_params=_params("parallel", "parallel"))(a, w, cos, sin)


def _gated_residual(y, x_ref, m_ref, *, n_x_rows, alpha):
    bm = y.shape[0]
    row = pl.program_id(0) * bm + lax.broadcasted_iota(jnp.int32, (bm, 1), 0)
    gate = jnp.where(row >= n_x_rows, m_ref[0, 1, 2:3, :], m_ref[0, 0, 2:3, :])
    return alpha * x_ref[...] + gate * y


def _mm_resid_body(a_ref, w_ref, x_ref, m_ref, o_ref, *, n_x_rows, alpha):
    y = jnp.dot(a_ref[...], w_ref[...], preferred_element_type=F32)
    o_ref[...] = _gated_residual(y, x_ref, m_ref, n_x_rows=n_x_rows, alpha=alpha)


def _matmul_resid(a, w, x, mods, *, layer, n_x_rows, alpha):
    m, k = a.shape
    n = w.shape[1]
    bm, bn = _pick(m, ROW_TILES[1:]), _pick(n, (1024, 512, 256, 128))
    return pl.pallas_call(
        functools.partial(_mm_resid_body, n_x_rows=n_x_rows, alpha=alpha), grid=(m // bm, n // bn),
        in_specs=[pl.BlockSpec((bm, k), lambda i, j: (i, 0)), pl.BlockSpec((k, bn), lambda i, j: (0, j)),
                  pl.BlockSpec((bm, bn), lambda i, j: (i, j)),
                  pl.BlockSpec((1, 2, 6, bn), lambda i, j: (layer, 0, 0, j))],
        out_specs=pl.BlockSpec((bm, bn), lambda i, j: (i, j)),
        out_shape=jax.ShapeDtypeStruct((m, n), F32), name="matmul_resid",
        compiler_params=_params("parallel", "parallel"))(a, w, x, mods)


def _mm_group_body(a_ref, w_ref, s_ref, x_ref, m_ref, o_ref, *, n_x_rows, alpha):
    ng, gk, gn = w_ref.shape
    y = jnp.concatenate(
        [jnp.dot(a_ref[:, g * gk:(g + 1) * gk], w_ref[g], preferred_element_type=F32) for g in range(ng)],
        axis=1)
    o_ref[...] = _gated_residual(y * s_ref[...], x_ref, m_ref, n_x_rows=n_x_rows, alpha=alpha)


def _matmul_groups_resid(a, w, ch_scale, x, mods, *, layer, n_x_rows, alpha):
    m, d = a.shape
    bm = _pick(m, (384, 256, 128))
    row = pl.BlockSpec((bm, d), lambda i: (i, 0))
    return pl.pallas_call(
        functools.partial(_mm_group_body, n_x_rows=n_x_rows, alpha=alpha), grid=(m // bm,),
        in_specs=[row, pl.BlockSpec(w.shape, lambda i: (0, 0, 0), pipeline_mode=pl.Buffered(1)),
                  pl.BlockSpec((1, d), lambda i: (0, 0)), row,
                  pl.BlockSpec((1, 2, 6, d), lambda i: (layer, 0, 0, 0))],
        out_specs=row,
        out_shape=jax.ShapeDtypeStruct((m, d), F32), name="pool_matmul_resid",
        compiler_params=_params("parallel"))(a, w, ch_scale.reshape(1, -1), x, mods)


def _mm_gate_body(a_ref, wb_ref, wc_ref, wx_ref, b_ref, v_ref):
    a = a_ref[...]
    b_ref[...] = jnp.dot(a, wb_ref[...], preferred_element_type=F32).astype(BF16)
    v_ref[...] = (jnp.dot(a, wc_ref[...], preferred_element_type=F32)
                  * jnp.dot(a, wx_ref[...], preferred_element_type=F32)).astype(BF16)


def _matmul_gates(a, w_in):
    m, k = a.shape
    d = w_in.shape[1] // 3
    bm = _pick(m, (768, 640, 512, 256, 128))
    bn = _pick(d, (512, 256, 128))
    nj = d // bn
    a_spec = pl.BlockSpec((bm, k), lambda i, j: (i, 0))
    out = pl.BlockSpec((bm, bn), lambda i, j: (i, j))
    return pl.pallas_call(
        _mm_gate_body, grid=(m // bm, nj),
        in_specs=[a_spec,
                  pl.BlockSpec((k, bn), lambda i, j: (0, j)),
                  pl.BlockSpec((k, bn), lambda i, j: (0, j + nj)),
                  pl.BlockSpec((k, bn), lambda i, j: (0, j + 2 * nj))],
        out_specs=[out, out],
        out_shape=[jax.ShapeDtypeStruct((m, d), BF16), jax.ShapeDtypeStruct((m, d), BF16)], name="conv_in_gates",
        compiler_params=_params("parallel", "parallel"))(a, w_in, w_in, w_in)


def _mlp_body(h_ref, w1_ref, w2_ref, *rest, n_side):
    src_refs, o_ref, dst_refs = rest[:n_side], rest[n_side], rest[n_side + 1:]

    def step(first):
        u = jnp.maximum(jnp.dot(h_ref[...], w1_ref[...], preferred_element_type=F32), 0.0)
        y = jnp.dot((u * u).astype(BF16), w2_ref[...], preferred_element_type=F32)
        if first:
            o_ref[...] = y
        else:
            o_ref[...] += y
        _cast_side(src_refs, dst_refs)

    pl.when(pl.program_id(1) == 0)(functools.partial(step, True))
    pl.when(pl.program_id(1) != 0)(functools.partial(step, False))


def _mlp(h, w1, w2, side=()):
    m, d = h.shape
    f = w1.shape[1]
    bm = _pick(m, ROW_TILES)
    bf = _pick(f, (512, 256, 128))
    n_f = f // bf
    out_mode = {} if 2 * bm * d * 4 <= VMEM_LIMIT // 2 else {"pipeline_mode": pl.Buffered(1)}
    side_in, side_out, side_shape, side_args = _side_casts(
        side, (m // bm) * n_f, lambda i, j: i * n_f + j)
    out = pl.pallas_call(
        functools.partial(_mlp_body, n_side=len(side)), grid=(m // bm, n_f),
        in_specs=[pl.BlockSpec((bm, d), lambda i, j: (i, 0), pipeline_mode=pl.Buffered(1)),
                  pl.BlockSpec((d, bf), lambda i, j: (0, j)),
                  pl.BlockSpec((bf, d), lambda i, j: (j, 0))] + side_in,
        out_specs=[pl.BlockSpec((bm, d), lambda i, j: (i, 0), **out_mode)] + side_out,
        out_shape=[jax.ShapeDtypeStruct((m, d), F32)] + side_shape, name="mlp",
        compiler_params=_params("arbitrary", "arbitrary"))(h, w1, w2, *side_args)
    return out[0], out[1:]


LOG2E = 1.4426950408889634


def _attn_body(sink_ref, bias_ref, q_ref, kp_ref, kc_ref, kn_ref, kx_ref, vp_ref, vc_ref, vn_ref, vx_ref,
               *rest, kv_per_step, n_side):
    src_refs, o_ref, dst_refs = rest[:n_side], rest[n_side], rest[n_side + 1:]
    _attn_heads(sink_ref, bias_ref, q_ref, kp_ref, kc_ref, kn_ref, kx_ref, vp_ref, vc_ref, vn_ref, vx_ref,
                o_ref, kv_per_step=kv_per_step)
    _cast_side(src_refs, dst_refs)


def _attn_heads(sink_ref, bias_ref, q_ref, kp_ref, kc_ref, kn_ref, kx_ref, vp_ref, vc_ref, vn_ref, vx_ref,
                o_ref, *, kv_per_step):
    scale = HEAD_DIM ** -0.5
    bias = bias_ref[0]
    for kv in range(kv_per_step):
        cols = slice(kv * HEAD_DIM, (kv + 1) * HEAD_DIM)
        heads = [(kv * KV_GROUP + g) for g in range(KV_GROUP)]
        q4 = jnp.concatenate([q_ref[:, hd * HEAD_DIM:(hd + 1) * HEAD_DIM] for hd in heads], axis=0)
        kcat = jnp.concatenate([kp_ref[:, cols], kc_ref[:, cols], kn_ref[:, cols], kx_ref[:, cols]], axis=0)
        vcat = jnp.concatenate([vp_ref[:, cols], vc_ref[:, cols], vn_ref[:, cols], vx_ref[:, cols]], axis=0)
        s = lax.dot_general(q4, kcat, (((1,), (1,)), ((), ())), preferred_element_type=F32)
        for g, hd in enumerate(heads):
            sg = s[g * ATT_BLOCK:(g + 1) * ATT_BLOCK] + bias
            sk = sink_ref[pl.program_id(0) * kv_per_step * KV_GROUP + hd] * (1.0 / scale)
            mx = jnp.maximum(jnp.max(sg, axis=1, keepdims=True), sk)
            p = jnp.exp2((sg - mx) * (scale * LOG2E))
            den = jnp.sum(p, axis=1, keepdims=True) + jnp.exp2((sk - mx) * (scale * LOG2E))
            og = jnp.dot(p.astype(BF16), vcat, preferred_element_type=F32)
            o_ref[:, hd * HEAD_DIM:(hd + 1) * HEAD_DIM] = (og / den).astype(o_ref.dtype)


def _band_bias(n_ctx_keys):
    blk = ATT_BLOCK
    nk = 3 * blk + n_ctx_keys
    qi = jnp.arange(blk)[:, None]
    kj = jnp.arange(nk)[None, :]
    rel = kj - blk - qi
    in_window = (jnp.abs(rel) <= blk) & (kj < 3 * blk)
    is_ctx_key = kj >= 3 * blk
    variants = []
    for v in range(5):
        ok = in_window
        if v & 1:
            ok = ok & (kj >= blk)
        if v & 2:
            ok = ok & (kj < 2 * blk)
        if v == 4:
            ok = jnp.zeros_like(in_window)
        variants.append(jnp.where(ok | is_ctx_key, 0.0, -jnp.inf))
    return jnp.stack(variants).astype(F32)


def _attention(qkv, sink, *, n_x_rows, n_heads, q_rows, side=()):
    t = qkv.shape[0]
    c = t - n_x_rows
    blk = ATT_BLOCK
    assert n_x_rows % c == 0 and c % blk == 0
    nkv = n_heads // KV_GROUP
    per = _pick(nkv, (4, 2, 1))
    nbx = n_x_rows // blk
    kcol = n_heads // per
    vcol = (n_heads + nkv) // per
    clampx = lambda b: jnp.clip(b, 0, nbx - 1)

    def band(col, shift):
        return pl.BlockSpec((blk, per * HEAD_DIM), lambda kh, n: (clampx(n + shift), col + kh))

    def ctx(col):
        return pl.BlockSpec((c, per * HEAD_DIM), lambda kh, n: (n_x_rows // c, col + kh))

    def variant(kh, n):
        edge = (n == 0).astype(jnp.int32) + 2 * (n == nbx - 1).astype(jnp.int32)
        return (jnp.where(n >= nbx, 4, edge), 0, 0)

    bias = _band_bias(c)
    qcols = per * KV_GROUP * HEAD_DIM
    n_q = q_rows // blk
    side_in, side_out, side_shape, side_args = _side_casts(
        side, (nkv // per) * n_q, lambda kh, n: kh * n_q + n)
    out = pl.pallas_call(
        functools.partial(_attn_body, kv_per_step=per, n_side=len(side)), grid=(nkv // per, n_q),
        in_specs=[pl.BlockSpec(memory_space=pltpu.SMEM),
                  pl.BlockSpec((1,) + bias.shape[1:], variant),
                  pl.BlockSpec((blk, qcols), lambda kh, n: (n, kh)),
                  band(kcol, -1), band(kcol, 0), band(kcol, 1), ctx(kcol),
                  band(vcol, -1), band(vcol, 0), band(vcol, 1), ctx(vcol)] + side_in,
        out_specs=[pl.BlockSpec((blk, qcols), lambda kh, n: (n, kh))] + side_out,
        out_shape=[jax.ShapeDtypeStruct((q_rows, n_heads * HEAD_DIM), BF16)] + side_shape, name="attention",
        compiler_params=_params("arbitrary", "arbitrary"))(sink, bias, qkv, *([qkv] * 8), *side_args)
    return out[0], out[1:]


def _rope_tables(n_x_rows, n_rows):
    tpos = jnp.arange(n_x_rows)
    n_freq = HEAD_DIM // 4
    freqs = ROPE_THETA ** (-jnp.arange(n_freq, dtype=F32) / n_freq)
    ang_r = (tpos // GRID_W).astype(F32)[:, None] * freqs
    ang_c = (tpos % GRID_W).astype(F32)[:, None] * freqs
    cos = jnp.concatenate([jnp.cos(ang_r)] * 2 + [jnp.cos(ang_c)] * 2, axis=1)
    sin = jnp.concatenate([-jnp.sin(ang_r), jnp.sin(ang_r), -jnp.sin(ang_c), jnp.sin(ang_c)], axis=1)
    pad = n_rows - n_x_rows
    cos = jnp.concatenate([cos, jnp.ones((pad, HEAD_DIM), F32)], axis=0)
    sin = jnp.concatenate([sin, jnp.zeros((pad, HEAD_DIM), F32)], axis=0)
    return cos, sin


def kernel(x, c, ctx, c_ctx, mod_w_down, mod_w_up, mod_b, ln_g, ln_b, mlp_w1, mlp_w2,
           attn_w_qkv, attn_w_o, attn_sink, pool_w, pool_scale,
           conv_w_in, conv_w, conv_b, conv_w_out):
    batch, s, d = x.shape
    assert batch == 1 and s % ROW_TILE == 0 and ctx.shape[1] % ROW_TILE == 0
    depth = mod_w_down.shape[0]
    t = s + ctx.shape[1]
    n_heads = d // HEAD_DIM
    alpha = (2 * depth) ** 0.25

    cond = jnp.zeros((8, d), F32).at[0].set(c[0]).at[1].set(c_ctx)
    cond = jnp.broadcast_to(cond, (depth, 8, d))
    mods = _vecmat(_vecmat(cond, mod_w_down, silu=True), mod_w_up, mod_b)
    mods = mods[:, :2].reshape(depth, 2, 6, d)

    cos, sin = _rope_tables(s, t)
    xs, h = _prep(x, ctx, mods)

    pool_w2d = pool_w.reshape(pool_w.shape[0], -1, pool_w.shape[-1])

    def layer_weights(i):
        kind, j = i % N_MIXERS, i // N_MIXERS
        mixer = {0: [("qkv", attn_w_qkv, j), ("wo", attn_w_o, j)],
                 1: [("pool", pool_w2d, j)],
                 2: [("cin", conv_w_in, j), ("cout", conv_w_out, j)]}[kind]
        return mixer + [("w1", mlp_w1, i), ("w2", mlp_w2, i)]

    wb = {name: w[idx].astype(BF16) for name, w, idx in layer_weights(0)[:-2]}

    for i in range(depth):
        last = i == depth - 1
        rows = s if last else t
        kind, j = i % N_MIXERS, i // N_MIXERS
        if kind == 0:
            n_rope = (n_heads + n_heads // KV_GROUP) * HEAD_DIM
            qkv = _matmul_rope(h, wb["qkv"], cos, sin, n_rope)
            own_mlp = [] if "w1" in wb else layer_weights(i)[-2:]
            o, cast = _attention(qkv, attn_sink[j], n_x_rows=s, n_heads=n_heads, q_rows=rows,
                                 side=[(w, idx) for _, w, idx in own_mlp])
            wb.update({name: wc for (name, _, _), wc in zip(own_mlp, cast)})
            z = _matmul_resid(o, wb["wo"], xs, mods, layer=i, n_x_rows=s, alpha=alpha)
        elif kind == 1:
            diffs = _pooldiff(xs, mods, layer=i, n_x_rows=s)
            z = _matmul_groups_resid(diffs, wb["pool"].reshape(pool_w.shape[1:]), pool_scale[j], xs, mods,
                                     layer=i, n_x_rows=s, alpha=alpha)
        else:
            b, v = _matmul_gates(h, wb["cin"])
            g = _gated_conv(b, v, conv_w, conv_b, layer=j, n_x_rows=s)
            z = _matmul_resid(g, wb["cout"], xs, mods, layer=i, n_x_rows=s, alpha=alpha)
        xs, h = _ln(z, None, mods, ln_g, ln_b, layer=i, which=0, next_layer=i, n_x_rows=s, rows=rows,
                    alpha=alpha, emit_h=True)
        for name, w, idx in layer_weights(i)[-2:]:
            if name not in wb:
                wb[name] = w[idx].astype(BF16)
        nxt = [] if last else layer_weights(i + 1)
        y, cast = _mlp(h, wb["w1"], wb["w2"], [(w, idx) for _, w, idx in nxt])
        wb = {name: wc for (name, _, _), wc in zip(nxt, cast)}
        need_h = (not last) and (i + 1) % N_MIXERS != 1
        xs, h = _ln(xs, y, mods, ln_g, ln_b, layer=i, which=1, next_layer=min(i + 1, depth - 1),
                    n_x_rows=s, rows=rows, alpha=alpha, emit_h=need_h)
    return xs.reshape(batch, s, d)
```

```python
import functools

import jax
import jax.numpy as jnp
from jax import lax
from jax.experimental import pallas as pl
from jax.experimental.pallas import tpu as pltpu

HEAD_DIM = 128
KV_GROUP = 4
GRID_W = 64
ATT_BLOCK = 128
ROPE_THETA = 10000.0
POOL_WINDOWS = (2, 4, 8, 16)
CONV_WIDTH = 3
N_MIXERS = 3
LN_EPS = 1e-5

LANES = 128
ROW_TILE = 256
HALO = 16
VMEM_LIMIT = 58 * 1024 * 1024

F32 = jnp.float32
BF16 = jnp.bfloat16


def _pick(n, candidates):
    for c in candidates:
        if n % c == 0:
            return c
    raise ValueError(f"no tile in {candidates} divides {n}")


def _params(*sem):
    return pltpu.CompilerParams(dimension_semantics=sem, vmem_limit_bytes=VMEM_LIMIT)


def _vecmat_body(a_ref, w_ref, o_ref, *, silu):
    a = a_ref[0]
    if silu:
        a = a * jax.nn.sigmoid(a)
    o_ref[0] = jnp.dot(a.astype(BF16), w_ref[0].astype(BF16), preferred_element_type=F32)


def _vecmat_bias_body(a_ref, w_ref, b_ref, o_ref):
    o_ref[0] = jnp.dot(a_ref[0].astype(BF16), w_ref[0].astype(BF16),
                       preferred_element_type=F32) + b_ref[0]


def _vecmat(a, w, bias=None, *, silu=False):
    nl, rows, k = a.shape
    n = w.shape[2]
    bn = _pick(n, (2048, 1024, 512, 256, 128))
    in_specs = [pl.BlockSpec((1, rows, k), lambda l, j: (l, 0, 0)),
                pl.BlockSpec((1, k, bn), lambda l, j: (l, 0, j))]
    args = [a, w]
    if bias is None:
        body = functools.partial(_vecmat_body, silu=silu)
    else:
        body = _vecmat_bias_body
        in_specs.append(pl.BlockSpec((1, 1, bn), lambda l, j: (l, 0, j)))
        args.append(bias.reshape(nl, 1, n))
    return pl.pallas_call(
        body, grid=(nl, n // bn), in_specs=in_specs,
        out_specs=pl.BlockSpec((1, rows, bn), lambda l, j: (l, 0, j)),
        out_shape=jax.ShapeDtypeStruct((nl, rows, n), F32), name="adaln_vecmat",
        compiler_params=_params("parallel", "parallel"))(*args)


def _prep_body(x_ref, c_ref, m_ref, xo_ref, ho_ref, *, n_x_tiles):
    is_ctx = pl.program_id(0) >= n_x_tiles
    xv = jnp.where(is_ctx, c_ref[0], x_ref[0])
    xo_ref[...] = xv
    ho_ref[...] = (xv * (1.0 + m_ref[0, 0, 1:2, :]) + m_ref[0, 0, 0:1, :]).astype(BF16)


def _prep(x, ctx, mods):
    _, s, d = x.shape
    c = ctx.shape[1]
    t = s + c
    rt = ROW_TILE
    nx = s // rt
    return pl.pallas_call(
        functools.partial(_prep_body, n_x_tiles=nx), grid=(t // rt,),
        in_specs=[pl.BlockSpec((1, rt, d), lambda i: (0, jnp.minimum(i, nx - 1), 0)),
                  pl.BlockSpec((1, rt, d), lambda i: (0, jnp.maximum(i - nx, 0), 0)),
                  pl.BlockSpec((1, 1, 6, d), lambda i: (0, (i >= nx).astype(jnp.int32), 0, 0))],
        out_specs=[pl.BlockSpec((rt, d), lambda i: (i, 0)), pl.BlockSpec((rt, d), lambda i: (i, 0))],
        out_shape=[jax.ShapeDtypeStruct((t, d), F32), jax.ShapeDtypeStruct((t, d), BF16)], name="prep",
        compiler_params=_params("parallel"))(x, ctx, mods)


def _ln_body(*refs, shift, scale, which, alpha):
    if alpha is None:
        z_ref, mn_ref, g_ref, b_ref, xo_ref, *ho_ref = refs
        z = z_ref[...]
    else:
        x_ref, y_ref, mg_ref, mn_ref, g_ref, b_ref, xo_ref, *ho_ref = refs
        gate = MIXER_GATE if which == 0 else MLP_GATE
        z = alpha * x_ref[...] + mg_ref[0, 0, gate:gate + 1, :] * y_ref[...]
    mu = jnp.mean(z, axis=-1, keepdims=True)
    zc = z - mu
    var = jnp.mean(zc * zc, axis=-1, keepdims=True)
    xn = zc * lax.rsqrt(var + LN_EPS) * g_ref[0, which:which + 1, :] + b_ref[0, which:which + 1, :]
    xo_ref[...] = xn
    if ho_ref:
        ho_ref[0][...] = (xn * (1.0 + mn_ref[0, 0, scale:scale + 1, :])
                          + mn_ref[0, 0, shift:shift + 1, :]).astype(BF16)


def _ln(z, mods, ln_g, ln_b, *, layer, which, next_layer, n_x_rows, rows, emit_h, resid=None):
    d = z.shape[1]
    rt = ROW_TILE
    nx = n_x_rows // rt
    shift, scale = (3, 4) if which == 0 else (0, 1)
    is_ctx = lambda i: (i >= nx).astype(jnp.int32)
    row = pl.BlockSpec((rt, d), lambda i: (i, 0))
    out_specs = [row]
    out_shape = [jax.ShapeDtypeStruct((rows, d), F32)]
    if emit_h:
        out_specs.append(row)
        out_shape.append(jax.ShapeDtypeStruct((rows, d), BF16))
    ln_spec = pl.BlockSpec((1, 2, d), lambda i: (layer, 0, 0))
    mod_spec = lambda l: pl.BlockSpec((1, 1, 6, d), lambda i: (l, is_ctx(i), 0, 0))
    if resid is None:
        alpha, in_specs, args = None, [row, mod_spec(next_layer), ln_spec, ln_spec], (z, mods, ln_g, ln_b)
    else:
        x, alpha = resid
        in_specs = [row, row, mod_spec(layer), mod_spec(next_layer), ln_spec, ln_spec]
        args = (x, z, mods, mods, ln_g, ln_b)
    out = pl.pallas_call(
        functools.partial(_ln_body, shift=shift, scale=scale, which=which, alpha=alpha),
        grid=(rows // rt,), in_specs=in_specs,
        out_specs=out_specs, out_shape=out_shape, name="layernorm",
        compiler_params=_params("parallel"))(*args)
    return (out[0], out[1]) if emit_h else (out[0], None)


def _seq_bounds(i, rt, n_x_rows, n_rows):
    in_x = i * rt < n_x_rows
    return jnp.where(in_x, 0, n_x_rows), jnp.where(in_x, n_x_rows, n_rows)


def _pooldiff_body(xp_ref, x_ref, xn_ref, m_ref, o_ref, *, n_x_rows, n_rows, group):
    rt = x_ref.shape[0]
    i = pl.program_id(0)
    lo, hi = _seq_bounds(i, rt, n_x_rows, n_rows)
    r = i * rt + lax.broadcasted_iota(jnp.int32, (rt, 1), 0)
    xx = jnp.concatenate([xp_ref[...], x_ref[...], xn_ref[...]], axis=0)
    hh = xx * (1.0 + m_ref[0, 0, 1:2, :]) + m_ref[0, 0, 0:1, :]
    rr = i * rt - HALO + lax.broadcasted_iota(jnp.int32, (rt + 2 * HALO, 1), 0)
    hm = jnp.where((rr >= lo) & (rr < hi), hh, 0.0)
    for g, w in enumerate(POOL_WINDOWS):
        assert w & (w - 1) == 0 and w // 2 <= HALO
        run = hm[:, g * group:(g + 1) * group]
        width = 1
        while width < w:
            run = run[:run.shape[0] - width] + run[width:]
            width *= 2
        first = HALO - w // 2
        cnt = (jnp.minimum(r + (w - w // 2), hi) - jnp.maximum(r - w // 2, lo)).astype(F32)
        o_ref[:, g * group:(g + 1) * group] = (
            run[first:first + rt] * (1.0 / cnt) - hh[HALO:HALO + rt, g * group:(g + 1) * group]).astype(BF16)


def _halo_specs(rt, d, n_rows):
    per = rt // HALO
    last = n_rows // HALO - 1
    return (pl.BlockSpec((HALO, d), lambda i: (jnp.maximum(i * per - 1, 0), 0)),
            pl.BlockSpec((rt, d), lambda i: (i, 0)),
            pl.BlockSpec((HALO, d), lambda i: (jnp.minimum((i + 1) * per, last), 0)))


def _pooldiff(x, mods, *, layer, n_x_rows):
    t, d = x.shape
    rt = ROW_TILE
    nx = n_x_rows // rt
    prev, cur, nxt = _halo_specs(rt, d, t)
    body = functools.partial(_pooldiff_body, n_x_rows=n_x_rows, n_rows=t, group=d // len(POOL_WINDOWS))
    return pl.pallas_call(
        body, grid=(t // rt,),
        in_specs=[prev, cur, nxt,
                  pl.BlockSpec((1, 1, 6, d), lambda i: (layer, (i >= nx).astype(jnp.int32), 0, 0))],
        out_specs=pl.BlockSpec((rt, d), lambda i: (i, 0)),
        out_shape=jax.ShapeDtypeStruct((t, d), BF16), name="pool_diff",
        compiler_params=_params("parallel"))(x, x, x, mods)


def _conv_body(b_ref, vp_ref, v_ref, vn_ref, w_ref, cb_ref, o_ref, *, n_x_rows, n_rows):
    rt = v_ref.shape[0]
    i = pl.program_id(0)
    lo, hi = _seq_bounds(i, rt, n_x_rows, n_rows)
    r = i * rt + lax.broadcasted_iota(jnp.int32, (rt, 1), 0)
    vv = jnp.concatenate([vp_ref[...], v_ref[...], vn_ref[...]], axis=0).astype(F32)
    pad = (CONV_WIDTH - 1) // 2
    acc = jnp.zeros(v_ref.shape, F32)
    for k in range(CONV_WIDTH):
        off = k - pad
        ok = (r + off >= lo) & (r + off < hi)
        acc = acc + jnp.where(ok, vv[HALO + off:HALO + off + rt], 0.0) * w_ref[0, k:k + 1, :]
    o_ref[...] = (b_ref[...].astype(F32) * (acc + cb_ref[0])).astype(BF16)


def _gated_conv(b, v, conv_w, conv_b, *, layer, n_x_rows):
    t, d = v.shape
    rt = ROW_TILE
    prev, cur, nxt = _halo_specs(rt, d, t)
    body = functools.partial(_conv_body, n_x_rows=n_x_rows, n_rows=t)
    return pl.pallas_call(
        body, grid=(t // rt,),
        in_specs=[cur, prev, cur, nxt,
                  pl.BlockSpec((1, CONV_WIDTH, d), lambda i: (layer, 0, 0)),
                  pl.BlockSpec((1, 1, d), lambda i: (layer, 0, 0))],
        out_specs=pl.BlockSpec((rt, d), lambda i: (i, 0)),
        out_shape=jax.ShapeDtypeStruct((t, d), BF16), name="gated_conv",
        compiler_params=_params("parallel"))(b, v, v, v, conv_w, conv_b.reshape(-1, 1, d))


ROW_TILES = (1024, 768, 640, 512, 256, 128)


def _mm_tiles(m, n):
    return _pick(m, ROW_TILES), _pick(n, (1024, 512, 256, 128))


def _side_chunks(rows, n_steps):
    for chunks in range(min(n_steps, rows // HALO), 0, -1):
        if rows % (chunks * HALO) == 0:
            return chunks
    raise ValueError(f"cannot chunk {rows} rows")


def _side_casts(side, n_steps, step_of):
    in_specs, out_specs, out_shape = [], [], []
    for w, layer in side:
        _, rows, cols = w.shape
        chunks = _side_chunks(rows, n_steps)
        chunk = lambda *ids, chunks=chunks: jnp.minimum(step_of(*ids), chunks - 1)
        in_specs.append(pl.BlockSpec((1, rows // chunks, cols),
                                     lambda *ids, layer=layer, chunk=chunk: (layer, chunk(*ids), 0)))
        out_specs.append(pl.BlockSpec((rows // chunks, cols), lambda *ids, chunk=chunk: (chunk(*ids), 0)))
        out_shape.append(jax.ShapeDtypeStruct((rows, cols), BF16))
    return in_specs, out_specs, out_shape, [w for w, _ in side]


def _cast_side(src_refs, dst_refs):
    for src, dst in zip(src_refs, dst_refs):
        dst[...] = src[0].astype(BF16)


def _mm_rope_body(a_ref, w_ref, c_ref, s_ref, o_ref, *, n_rope_tiles):
    rotated = pl.program_id(1) < n_rope_tiles
    bm, bn = o_ref.shape
    quarter = HEAD_DIM // 4
    reps = bn // HEAD_DIM
    for rows in (slice(0, bm // 2), slice(bm // 2, bm)):
        acc = jnp.dot(a_ref[rows, :], w_ref[...], preferred_element_type=F32)
        lane = lax.broadcasted_iota(jnp.int32, acc.shape, 1)
        partner = jnp.where((lane & quarter) == 0,
                            pltpu.roll(acc, bn - quarter, 1), pltpu.roll(acc, quarter, 1))
        cos = jnp.concatenate([jnp.where(rotated, c_ref[rows, :], 1.0)] * reps, axis=1)
        sin = jnp.concatenate([jnp.where(rotated, s_ref[rows, :], 0.0)] * reps, axis=1)
        o_ref[rows, :] = (acc * cos + partner * sin).astype(o_ref.dtype)


def _matmul_rope(a, w, cos, sin, n_rope_cols):
    m, k = a.shape
    n = w.shape[1]
    bm, _ = _mm_tiles(m, n)
    bn = _pick(n - n_rope_cols, (1024, 512, 256, 128))
    assert n_rope_cols % bn == 0
    body = functools.partial(_mm_rope_body, n_rope_tiles=n_rope_cols // bn)
    return pl.pallas_call(
        body, grid=(m // bm, n // bn),
        in_specs=[pl.BlockSpec((bm, k), lambda i, j: (i, 0)), pl.BlockSpec((k, bn), lambda i, j: (0, j)),
                  pl.BlockSpec((bm, HEAD_DIM), lambda i, j: (i, 0)),
                  pl.BlockSpec((bm, HEAD_DIM), lambda i, j: (i, 0))],
        out_specs=pl.BlockSpec((bm, bn), lambda i, j: (i, j)),
        out_shape=jax.ShapeDtypeStruct((m, n), BF16), name="qkv_rope",
        compiler_params=_params("parallel", "parallel"))(a, w, cos, sin)


MIXER_GATE, MLP_GATE = 2, 5


def _row_gate(m_ref, which, bm, n_x_rows):
    row = pl.program_id(0) * bm + lax.broadcasted_iota(jnp.int32, (bm, 1), 0)
    return jnp.where(row >= n_x_rows, m_ref[0, 1, which:which + 1, :], m_ref[0, 0, which:which + 1, :])


def _gated_residual(y, x_ref, m_ref, *, n_x_rows, alpha):
    return alpha * x_ref[...] + _row_gate(m_ref, MIXER_GATE, y.shape[0], n_x_rows) * y


def _mm_resid_body(a_ref, w_ref, x_ref, m_ref, o_ref, *, n_x_rows, alpha):
    y = jnp.dot(a_ref[...], w_ref[...], preferred_element_type=F32)
    o_ref[...] = _gated_residual(y, x_ref, m_ref, n_x_rows=n_x_rows, alpha=alpha)


def _matmul_resid(a, w, x, mods, *, layer, n_x_rows, alpha):
    m, k = a.shape
    n = w.shape[1]
    bm, bn = _mm_tiles(m, n)
    return pl.pallas_call(
        functools.partial(_mm_resid_body, n_x_rows=n_x_rows, alpha=alpha), grid=(m // bm, n // bn),
        in_specs=[pl.BlockSpec((bm, k), lambda i, j: (i, 0)), pl.BlockSpec((k, bn), lambda i, j: (0, j)),
                  pl.BlockSpec((bm, bn), lambda i, j: (i, j)),
                  pl.BlockSpec((1, 2, 6, bn), lambda i, j: (layer, 0, 0, j))],
        out_specs=pl.BlockSpec((bm, bn), lambda i, j: (i, j)),
        out_shape=jax.ShapeDtypeStruct((m, n), F32), name="matmul_resid",
        compiler_params=_params("parallel", "parallel"))(a, w, x, mods)


def _mm_group_body(a_ref, w_ref, s_ref, x_ref, m_ref, o_ref, *, n_x_rows, alpha):
    ng, gk, gn = w_ref.shape
    y = jnp.concatenate(
        [jnp.dot(a_ref[:, g * gk:(g + 1) * gk], w_ref[g], preferred_element_type=F32) for g in range(ng)],
        axis=1)
    o_ref[...] = _gated_residual(y * s_ref[...], x_ref, m_ref, n_x_rows=n_x_rows, alpha=alpha)


def _matmul_groups_resid(a, w, ch_scale, x, mods, *, layer, n_x_rows, alpha):
    m, d = a.shape
    bm = _pick(m, (384, 256, 128))
    row = pl.BlockSpec((bm, d), lambda i: (i, 0))
    return pl.pallas_call(
        functools.partial(_mm_group_body, n_x_rows=n_x_rows, alpha=alpha), grid=(m // bm,),
        in_specs=[row, pl.BlockSpec(w.shape, lambda i: (0, 0, 0), pipeline_mode=pl.Buffered(1)),
                  pl.BlockSpec((1, d), lambda i: (0, 0)), row,
                  pl.BlockSpec((1, 2, 6, d), lambda i: (layer, 0, 0, 0))],
        out_specs=row,
        out_shape=jax.ShapeDtypeStruct((m, d), F32), name="pool_matmul_resid",
        compiler_params=_params("parallel"))(a, w, ch_scale.reshape(1, -1), x, mods)


def _mm_gate_body(a_ref, wb_ref, wc_ref, wx_ref, b_ref, v_ref):
    a = a_ref[...]
    b_ref[...] = jnp.dot(a, wb_ref[...], preferred_element_type=F32).astype(BF16)
    v_ref[...] = (jnp.dot(a, wc_ref[...], preferred_element_type=F32)
                  * jnp.dot(a, wx_ref[...], preferred_element_type=F32)).astype(BF16)


def _matmul_gates(a, w_in):
    m, k = a.shape
    d = w_in.shape[1] // 3
    bm = _pick(m, (768, 640, 512, 256, 128))
    bn = _pick(d, (512, 256, 128))
    nj = d // bn
    a_spec = pl.BlockSpec((bm, k), lambda i, j: (i, 0))
    out = pl.BlockSpec((bm, bn), lambda i, j: (i, j))
    return pl.pallas_call(
        _mm_gate_body, grid=(m // bm, nj),
        in_specs=[a_spec,
                  pl.BlockSpec((k, bn), lambda i, j: (0, j)),
                  pl.BlockSpec((k, bn), lambda i, j: (0, j + nj)),
                  pl.BlockSpec((k, bn), lambda i, j: (0, j + 2 * nj))],
        out_specs=[out, out],
        out_shape=[jax.ShapeDtypeStruct((m, d), BF16), jax.ShapeDtypeStruct((m, d), BF16)], name="conv_in_gates",
        compiler_params=_params("parallel", "parallel"))(a, w_in, w_in, w_in)


def _mlp_body(h_ref, w1_ref, w2_ref, *rest, n_side):
    src_refs, o_ref, dst_refs = rest[:n_side], rest[n_side], rest[n_side + 1:]

    def step(first):
        u = jnp.maximum(jnp.dot(h_ref[...], w1_ref[...], preferred_element_type=F32), 0.0)
        y = jnp.dot((u * u).astype(BF16), w2_ref[...], preferred_element_type=F32)
        if first:
            o_ref[...] = y
        else:
            o_ref[...] += y
        _cast_side(src_refs, dst_refs)

    pl.when(pl.program_id(1) == 0)(functools.partial(step, True))
    pl.when(pl.program_id(1) != 0)(functools.partial(step, False))


def _mlp(h, w1, w2, side=()):
    m, d = h.shape
    f = w1.shape[1]
    bm = _pick(m, ROW_TILES)
    bf = _pick(f, (512, 256, 128))
    n_f = f // bf
    out_mode = {} if 2 * bm * d * 4 <= VMEM_LIMIT // 2 else {"pipeline_mode": pl.Buffered(1)}
    side_in, side_out, side_shape, side_args = _side_casts(
        side, (m // bm) * n_f, lambda i, j: i * n_f + j)
    out = pl.pallas_call(
        functools.partial(_mlp_body, n_side=len(side)), grid=(m // bm, n_f),
        in_specs=[pl.BlockSpec((bm, d), lambda i, j: (i, 0), pipeline_mode=pl.Buffered(1)),
                  pl.BlockSpec((d, bf), lambda i, j: (0, j)),
                  pl.BlockSpec((bf, d), lambda i, j: (j, 0))] + side_in,
        out_specs=[pl.BlockSpec((bm, d), lambda i, j: (i, 0), **out_mode)] + side_out,
        out_shape=[jax.ShapeDtypeStruct((m, d), F32)] + side_shape, name="mlp",
        compiler_params=_params("arbitrary", "arbitrary"))(h, w1, w2, *side_args)
    return out[0], out[1:]


LOG2E = 1.4426950408889634


def _attn_body(sink_ref, bias_ref, q_ref, kp_ref, kc_ref, kn_ref, kx_ref, vp_ref, vc_ref, vn_ref, vx_ref,
               *rest, kv_per_step, n_side):
    src_refs, o_ref, dst_refs = rest[:n_side], rest[n_side], rest[n_side + 1:]
    _attn_heads(sink_ref, bias_ref, q_ref, kp_ref, kc_ref, kn_ref, kx_ref, vp_ref, vc_ref, vn_ref, vx_ref,
                o_ref, kv_per_step=kv_per_step)
    _cast_side(src_refs, dst_refs)


def _attn_heads(sink_ref, bias_ref, q_ref, kp_ref, kc_ref, kn_ref, kx_ref, vp_ref, vc_ref, vn_ref, vx_ref,
                o_ref, *, kv_per_step):
    scale = HEAD_DIM ** -0.5
    bias = bias_ref[0]
    for kv in range(kv_per_step):
        cols = slice(kv * HEAD_DIM, (kv + 1) * HEAD_DIM)
        heads = [(kv * KV_GROUP + g) for g in range(KV_GROUP)]
        q4 = jnp.concatenate([q_ref[:, hd * HEAD_DIM:(hd + 1) * HEAD_DIM] for hd in heads], axis=0)
        kcat = jnp.concatenate([kp_ref[:, cols], kc_ref[:, cols], kn_ref[:, cols], kx_ref[:, cols]], axis=0)
        vcat = jnp.concatenate([vp_ref[:, cols], vc_ref[:, cols], vn_ref[:, cols], vx_ref[:, cols]], axis=0)
        s = lax.dot_general(q4, kcat, (((1,), (1,)), ((), ())), preferred_element_type=F32)
        for g, hd in enumerate(heads):
            sg = s[g * ATT_BLOCK:(g + 1) * ATT_BLOCK] + bias
            sk = sink_ref[pl.program_id(0) * kv_per_step * KV_GROUP + hd] * (1.0 / scale)
            mx = jnp.maximum(jnp.max(sg, axis=1, keepdims=True), sk)
            p = jnp.exp2((sg - mx) * (scale * LOG2E))
            den = jnp.sum(p, axis=1, keepdims=True) + jnp.exp2((sk - mx) * (scale * LOG2E))
            og = jnp.dot(p.astype(BF16), vcat, preferred_element_type=F32)
            o_ref[:, hd * HEAD_DIM:(hd + 1) * HEAD_DIM] = (og / den).astype(o_ref.dtype)


def _band_bias(n_ctx_keys):
    blk = ATT_BLOCK
    nk = 3 * blk + n_ctx_keys
    qi = jnp.arange(blk)[:, None]
    kj = jnp.arange(nk)[None, :]
    rel = kj - blk - qi
    in_window = (jnp.abs(rel) <= blk) & (kj < 3 * blk)
    is_ctx_key = kj >= 3 * blk
    variants = []
    for v in range(5):
        ok = in_window
        if v & 1:
            ok = ok & (kj >= blk)
        if v & 2:
            ok = ok & (kj < 2 * blk)
        if v == 4:
            ok = jnp.zeros_like(in_window)
        variants.append(jnp.where(ok | is_ctx_key, 0.0, -jnp.inf))
    return jnp.stack(variants).astype(F32)


def _attention(qkv, sink, *, n_x_rows, n_heads, q_rows, side=()):
    t = qkv.shape[0]
    c = t - n_x_rows
    blk = ATT_BLOCK
    assert n_x_rows % c == 0 and c % blk == 0
    nkv = n_heads // KV_GROUP
    per = _pick(nkv, (4, 2, 1))
    nbx = n_x_rows // blk
    kcol = n_heads // per
    vcol = (n_heads + nkv) // per
    clampx = lambda b: jnp.clip(b, 0, nbx - 1)

    def band(col, shift):
        return pl.BlockSpec((blk, per * HEAD_DIM), lambda kh, n: (clampx(n + shift), col + kh))

    def ctx(col):
        return pl.BlockSpec((c, per * HEAD_DIM), lambda kh, n: (n_x_rows // c, col + kh))

    def variant(kh, n):
        edge = (n == 0).astype(jnp.int32) + 2 * (n == nbx - 1).astype(jnp.int32)
        return (jnp.where(n >= nbx, 4, edge), 0, 0)

    bias = _band_bias(c)
    qcols = per * KV_GROUP * HEAD_DIM
    n_q = q_rows // blk
    side_in, side_out, side_shape, side_args = _side_casts(
        side, (nkv // per) * n_q, lambda kh, n: kh * n_q + n)
    out = pl.pallas_call(
        functools.partial(_attn_body, kv_per_step=per, n_side=len(side)), grid=(nkv // per, n_q),
        in_specs=[pl.BlockSpec(memory_space=pltpu.SMEM),
                  pl.BlockSpec((1,) + bias.shape[1:], variant),
                  pl.BlockSpec((blk, qcols), lambda kh, n: (n, kh)),
                  band(kcol, -1), band(kcol, 0), band(kcol, 1), ctx(kcol),
                  band(vcol, -1), band(vcol, 0), band(vcol, 1), ctx(vcol)] + side_in,
        out_specs=[pl.BlockSpec((blk, qcols), lambda kh, n: (n, kh))] + side_out,
        out_shape=[jax.ShapeDtypeStruct((q_rows, n_heads * HEAD_DIM), BF16)] + side_shape, name="attention",
        compiler_params=_params("arbitrary", "arbitrary"))(sink, bias, qkv, *([qkv] * 8), *side_args)
    return out[0], out[1:]


def _rope_tables(n_x_rows, n_rows):
    tpos = jnp.arange(n_x_rows)
    n_freq = HEAD_DIM // 4
    freqs = ROPE_THETA ** (-jnp.arange(n_freq, dtype=F32) / n_freq)
    ang_r = (tpos // GRID_W).astype(F32)[:, None] * freqs
    ang_c = (tpos % GRID_W).astype(F32)[:, None] * freqs
    cos = jnp.concatenate([jnp.cos(ang_r)] * 2 + [jnp.cos(ang_c)] * 2, axis=1)
    sin = jnp.concatenate([-jnp.sin(ang_r), jnp.sin(ang_r), -jnp.sin(ang_c), jnp.sin(ang_c)], axis=1)
    pad = n_rows - n_x_rows
    cos = jnp.concatenate([cos, jnp.ones((pad, HEAD_DIM), F32)], axis=0)
    sin = jnp.concatenate([sin, jnp.zeros((pad, HEAD_DIM), F32)], axis=0)
    return cos, sin


def kernel(x, c, ctx, c_ctx, mod_w_down, mod_w_up, mod_b, ln_g, ln_b, mlp_w1, mlp_w2,
           attn_w_qkv, attn_w_o, attn_sink, pool_w, pool_scale,
           conv_w_in, conv_w, conv_b, conv_w_out):
    batch, s, d = x.shape
    assert batch == 1 and s % ROW_TILE == 0 and ctx.shape[1] % ROW_TILE == 0
    depth = mod_w_down.shape[0]
    t = s + ctx.shape[1]
    n_heads = d // HEAD_DIM
    alpha = (2 * depth) ** 0.25

    cond = jnp.zeros((8, d), F32).at[0].set(c[0]).at[1].set(c_ctx)
    cond = jnp.broadcast_to(cond, (depth, 8, d))
    mods = _vecmat(_vecmat(cond, mod_w_down, silu=True), mod_w_up, mod_b)
    mods = mods[:, :2].reshape(depth, 2, 6, d)

    cos, sin = _rope_tables(s, t)
    xs, h = _prep(x, ctx, mods)

    pool_w2d = pool_w.reshape(pool_w.shape[0], -1, pool_w.shape[-1])

    def layer_weights(i):
        kind, j = i % N_MIXERS, i // N_MIXERS
        mixer = {0: [("qkv", attn_w_qkv, j), ("wo", attn_w_o, j)],
                 1: [("pool", pool_w2d, j)],
                 2: [("cin", conv_w_in, j), ("cout", conv_w_out, j)]}[kind]
        return mixer + [("w1", mlp_w1, i), ("w2", mlp_w2, i)]

    wb = {name: w[idx].astype(BF16) for name, w, idx in layer_weights(0)[:-2]}

    for i in range(depth):
        last = i == depth - 1
        rows = s if last else t
        kind, j = i % N_MIXERS, i // N_MIXERS
        if kind == 0:
            n_rope = (n_heads + n_heads // KV_GROUP) * HEAD_DIM
            qkv = _matmul_rope(h, wb["qkv"], cos, sin, n_rope)
            own_mlp = [] if "w1" in wb else layer_weights(i)[-2:]
            o, cast = _attention(qkv, attn_sink[j], n_x_rows=s, n_heads=n_heads, q_rows=rows,
                                 side=[(w, idx) for _, w, idx in own_mlp])
            wb.update({name: wc for (name, _, _), wc in zip(own_mlp, cast)})
            z = _matmul_resid(o, wb["wo"], xs, mods, layer=i, n_x_rows=s, alpha=alpha)
        elif kind == 1:
            diffs = _pooldiff(xs, mods, layer=i, n_x_rows=s)
            z = _matmul_groups_resid(diffs, wb["pool"].reshape(pool_w.shape[1:]), pool_scale[j], xs, mods,
                                     layer=i, n_x_rows=s, alpha=alpha)
        else:
            b, v = _matmul_gates(h, wb["cin"])
            g = _gated_conv(b, v, conv_w, conv_b, layer=j, n_x_rows=s)
            z = _matmul_resid(g, wb["cout"], xs, mods, layer=i, n_x_rows=s, alpha=alpha)
        xs, h = _ln(z, mods, ln_g, ln_b, layer=i, which=0, next_layer=i, n_x_rows=s, rows=rows, emit_h=True)
        for name, w, idx in layer_weights(i)[-2:]:
            if name not in wb:
                wb[name] = w[idx].astype(BF16)
        nxt = [] if last else layer_weights(i + 1)
        y, cast = _mlp(h, wb["w1"], wb["w2"], [(w, idx) for _, w, idx in nxt])
        wb = {name: wc for (name, _, _), wc in zip(nxt, cast)}
        need_h = (not last) and (i + 1) % N_MIXERS != 1
        xs, h = _ln(y, mods, ln_g, ln_b, layer=i, which=1, next_layer=min(i + 1, depth - 1),
                    n_x_rows=s, rows=rows, emit_h=need_h, resid=(xs, alpha))
    return xs.reshape(batch, s, d)
```

```python
import functools

import jax
import jax.numpy as jnp
from jax import lax
from jax.experimental import pallas as pl
from jax.experimental.pallas import tpu as pltpu

HEAD_DIM = 128
KV_GROUP = 4
GRID_W = 64
ATT_BLOCK = 128
ROPE_THETA = 10000.0
POOL_WINDOWS = (2, 4, 8, 16)
CONV_WIDTH = 3
N_MIXERS = 3
LN_EPS = 1e-5

LANES = 128
ROW_TILE = 256
HALO = 16
VMEM_LIMIT = 58 * 1024 * 1024

F32 = jnp.float32
BF16 = jnp.bfloat16


def _pick(n, candidates):
    for c in candidates:
        if n % c == 0:
            return c
    raise ValueError(f"no tile in {candidates} divides {n}")


def _params(*sem):
    return pltpu.CompilerParams(dimension_semantics=sem, vmem_limit_bytes=VMEM_LIMIT)


def _vecmat_body(a_ref, w_ref, o_ref, *, silu):
    a = a_ref[0]
    if silu:
        a = a * jax.nn.sigmoid(a)
    o_ref[0] = jnp.dot(a.astype(BF16), w_ref[0].astype(BF16), preferred_element_type=F32)


def _vecmat_bias_body(a_ref, w_ref, b_ref, o_ref):
    o_ref[0] = jnp.dot(a_ref[0].astype(BF16), w_ref[0].astype(BF16),
                       preferred_element_type=F32) + b_ref[0]


def _vecmat(a, w, bias=None, *, silu=False):
    nl, rows, k = a.shape
    n = w.shape[2]
    bn = _pick(n, (2048, 1024, 512, 256, 128))
    in_specs = [pl.BlockSpec((1, rows, k), lambda l, j: (l, 0, 0)),
                pl.BlockSpec((1, k, bn), lambda l, j: (l, 0, j))]
    args = [a, w]
    if bias is None:
        body = functools.partial(_vecmat_body, silu=silu)
    else:
        body = _vecmat_bias_body
        in_specs.append(pl.BlockSpec((1, 1, bn), lambda l, j: (l, 0, j)))
        args.append(bias.reshape(nl, 1, n))
    return pl.pallas_call(
        body, grid=(nl, n // bn), in_specs=in_specs,
        out_specs=pl.BlockSpec((1, rows, bn), lambda l, j: (l, 0, j)),
        out_shape=jax.ShapeDtypeStruct((nl, rows, n), F32), name="adaln_vecmat",
        compiler_params=_params("parallel", "parallel"))(*args)


def _prep_body(x_ref, c_ref, m_ref, xo_ref, ho_ref, *, n_x_tiles):
    is_ctx = pl.program_id(0) >= n_x_tiles
    xv = jnp.where(is_ctx, c_ref[0], x_ref[0])
    xo_ref[...] = xv
    ho_ref[...] = (xv * (1.0 + m_ref[0, 0, 1:2, :]) + m_ref[0, 0, 0:1, :]).astype(BF16)


def _prep(x, ctx, mods):
    _, s, d = x.shape
    c = ctx.shape[1]
    t = s + c
    rt = ROW_TILE
    nx = s // rt
    return pl.pallas_call(
        functools.partial(_prep_body, n_x_tiles=nx), grid=(t // rt,),
        in_specs=[pl.BlockSpec((1, rt, d), lambda i: (0, jnp.minimum(i, nx - 1), 0)),
                  pl.BlockSpec((1, rt, d), lambda i: (0, jnp.maximum(i - nx, 0), 0)),
                  pl.BlockSpec((1, 1, 6, d), lambda i: (0, (i >= nx).astype(jnp.int32), 0, 0))],
        out_specs=[pl.BlockSpec((rt, d), lambda i: (i, 0)), pl.BlockSpec((rt, d), lambda i: (i, 0))],
        out_shape=[jax.ShapeDtypeStruct((t, d), F32), jax.ShapeDtypeStruct((t, d), BF16)], name="prep",
        compiler_params=_params("parallel"))(x, ctx, mods)


def _ln_body(*refs, shift, scale, which, alpha):
    if alpha is None:
        z_ref, mn_ref, g_ref, b_ref, xo_ref, *ho_ref = refs
        z = z_ref[...]
    else:
        x_ref, y_ref, mg_ref, mn_ref, g_ref, b_ref, xo_ref, *ho_ref = refs
        gate = MIXER_GATE if which == 0 else MLP_GATE
        z = alpha * x_ref[...] + mg_ref[0, 0, gate:gate + 1, :] * y_ref[...]
    mu = jnp.mean(z, axis=-1, keepdims=True)
    zc = z - mu
    var = jnp.mean(zc * zc, axis=-1, keepdims=True)
    xn = zc * lax.rsqrt(var + LN_EPS) * g_ref[0, which:which + 1, :] + b_ref[0, which:which + 1, :]
    xo_ref[...] = xn
    if ho_ref:
        ho_ref[0][...] = (xn * (1.0 + mn_ref[0, 0, scale:scale + 1, :])
                          + mn_ref[0, 0, shift:shift + 1, :]).astype(BF16)


def _ln(z, mods, ln_g, ln_b, *, layer, which, next_layer, n_x_rows, rows, emit_h, resid=None):
    d = z.shape[1]
    rt = ROW_TILE
    nx = n_x_rows // rt
    shift, scale = (3, 4) if which == 0 else (0, 1)
    is_ctx = lambda i: (i >= nx).astype(jnp.int32)
    row = pl.BlockSpec((rt, d), lambda i: (i, 0))
    out_specs = [row]
    out_shape = [jax.ShapeDtypeStruct((rows, d), F32)]
    if emit_h:
        out_specs.append(row)
        out_shape.append(jax.ShapeDtypeStruct((rows, d), BF16))
    ln_spec = pl.BlockSpec((1, 2, d), lambda i: (layer, 0, 0))
    mod_spec = lambda l: pl.BlockSpec((1, 1, 6, d), lambda i: (l, is_ctx(i), 0, 0))
    if resid is None:
        alpha, in_specs, args = None, [row, mod_spec(next_layer), ln_spec, ln_spec], (z, mods, ln_g, ln_b)
    else:
        x, alpha = resid
        in_specs = [row, row, mod_spec(layer), mod_spec(next_layer), ln_spec, ln_spec]
        args = (x, z, mods, mods, ln_g, ln_b)
    out = pl.pallas_call(
        functools.partial(_ln_body, shift=shift, scale=scale, which=which, alpha=alpha),
        grid=(rows // rt,), in_specs=in_specs,
        out_specs=out_specs, out_shape=out_shape, name="layernorm",
        compiler_params=_params("parallel"))(*args)
    return (out[0], out[1]) if emit_h else (out[0], None)


def _seq_bounds(i, rt, n_x_rows, n_rows):
    in_x = i * rt < n_x_rows
    return jnp.where(in_x, 0, n_x_rows), jnp.where(in_x, n_x_rows, n_rows)


def _pool_body(xp_ref, x_ref, xn_ref, m_ref, w_ref, s_ref, o_ref, *, n_x_rows, n_rows, alpha):
    rt = x_ref.shape[0]
    n_groups, group, _ = w_ref.shape
    i = pl.program_id(0)
    lo, hi = _seq_bounds(i, rt, n_x_rows, n_rows)
    r = i * rt + lax.broadcasted_iota(jnp.int32, (rt, 1), 0)
    xx = jnp.concatenate([xp_ref[...], x_ref[...], xn_ref[...]], axis=0)
    hh = xx * (1.0 + m_ref[0, 0, 1:2, :]) + m_ref[0, 0, 0:1, :]
    rr = i * rt - HALO + lax.broadcasted_iota(jnp.int32, (rt + 2 * HALO, 1), 0)
    hm = jnp.where((rr >= lo) & (rr < hi), hh, 0.0)
    for g, w in enumerate(POOL_WINDOWS):
        assert w & (w - 1) == 0 and w // 2 <= HALO
        cols = slice(g * group, (g + 1) * group)
        run = hm[:, cols]
        width = 1
        while width < w:
            run = run[:run.shape[0] - width] + run[width:]
            width *= 2
        first = HALO - w // 2
        cnt = (jnp.minimum(r + (w - w // 2), hi) - jnp.maximum(r - w // 2, lo)).astype(F32)
        diff = (run[first:first + rt] * (1.0 / cnt) - hh[HALO:HALO + rt, cols]).astype(BF16)
        y = jnp.dot(diff, w_ref[g], preferred_element_type=F32) * s_ref[:, cols]
        o_ref[:, cols] = alpha * x_ref[:, cols] + m_ref[0, 0, MIXER_GATE:MIXER_GATE + 1, cols] * y


def _halo_specs(rt, d, n_rows):
    per = rt // HALO
    last = n_rows // HALO - 1
    return (pl.BlockSpec((HALO, d), lambda i: (jnp.maximum(i * per - 1, 0), 0)),
            pl.BlockSpec((rt, d), lambda i: (i, 0)),
            pl.BlockSpec((HALO, d), lambda i: (jnp.minimum((i + 1) * per, last), 0)))


def _pool_mixer(x, w, ch_scale, mods, *, layer, n_x_rows, alpha):
    t, d = x.shape
    rt = ROW_TILE
    nx = n_x_rows // rt
    prev, cur, nxt = _halo_specs(rt, d, t)
    body = functools.partial(_pool_body, n_x_rows=n_x_rows, n_rows=t, alpha=alpha)
    return pl.pallas_call(
        body, grid=(t // rt,),
        in_specs=[prev, cur, nxt,
                  pl.BlockSpec((1, 1, 6, d), lambda i: (layer, (i >= nx).astype(jnp.int32), 0, 0)),
                  pl.BlockSpec(w.shape, lambda i: (0, 0, 0), pipeline_mode=pl.Buffered(1)),
                  pl.BlockSpec((1, d), lambda i: (0, 0))],
        out_specs=pl.BlockSpec((rt, d), lambda i: (i, 0)),
        out_shape=jax.ShapeDtypeStruct((t, d), F32), name="pool_mixer",
        compiler_params=_params("parallel"))(x, x, x, mods, w, ch_scale.reshape(1, -1))


def _conv_body(b_ref, vp_ref, v_ref, vn_ref, w_ref, cb_ref, o_ref, *, n_x_rows, n_rows):
    rt = v_ref.shape[0]
    i = pl.program_id(0)
    lo, hi = _seq_bounds(i, rt, n_x_rows, n_rows)
    r = i * rt + lax.broadcasted_iota(jnp.int32, (rt, 1), 0)
    vv = jnp.concatenate([vp_ref[...], v_ref[...], vn_ref[...]], axis=0).astype(F32)
    pad = (CONV_WIDTH - 1) // 2
    acc = jnp.zeros(v_ref.shape, F32)
    for k in range(CONV_WIDTH):
        off = k - pad
        ok = (r + off >= lo) & (r + off < hi)
        acc = acc + jnp.where(ok, vv[HALO + off:HALO + off + rt], 0.0) * w_ref[0, k:k + 1, :]
    o_ref[...] = (b_ref[...].astype(F32) * (acc + cb_ref[0])).astype(BF16)


def _gated_conv(b, v, conv_w, conv_b, *, layer, n_x_rows):
    t, d = v.shape
    rt = ROW_TILE
    prev, cur, nxt = _halo_specs(rt, d, t)
    body = functools.partial(_conv_body, n_x_rows=n_x_rows, n_rows=t)
    return pl.pallas_call(
        body, grid=(t // rt,),
        in_specs=[cur, prev, cur, nxt,
                  pl.BlockSpec((1, CONV_WIDTH, d), lambda i: (layer, 0, 0)),
                  pl.BlockSpec((1, 1, d), lambda i: (layer, 0, 0))],
        out_specs=pl.BlockSpec((rt, d), lambda i: (i, 0)),
        out_shape=jax.ShapeDtypeStruct((t, d), BF16), name="gated_conv",
        compiler_params=_params("parallel"))(b, v, v, v, conv_w, conv_b.reshape(-1, 1, d))


ROW_TILES = (1024, 768, 640, 512, 256, 128)


def _mm_tiles(m, n):
    return _pick(m, ROW_TILES), _pick(n, (1024, 512, 256, 128))


def _side_chunks(rows, n_steps):
    for chunks in range(min(n_steps, rows // HALO), 0, -1):
        if rows % (chunks * HALO) == 0:
            return chunks
    raise ValueError(f"cannot chunk {rows} rows")


def _side_casts(side, n_steps, step_of):
    in_specs, out_specs, out_shape = [], [], []
    for w, layer in side:
        _, rows, cols = w.shape
        chunks = _side_chunks(rows, n_steps)
        chunk = lambda *ids, chunks=chunks: jnp.minimum(step_of(*ids), chunks - 1)
        in_specs.append(pl.BlockSpec((1, rows // chunks, cols),
                                     lambda *ids, layer=layer, chunk=chunk: (layer, chunk(*ids), 0)))
        out_specs.append(pl.BlockSpec((rows // chunks, cols), lambda *ids, chunk=chunk: (chunk(*ids), 0)))
        out_shape.append(jax.ShapeDtypeStruct((rows, cols), BF16))
    return in_specs, out_specs, out_shape, [w for w, _ in side]


def _cast_side(src_refs, dst_refs):
    for src, dst in zip(src_refs, dst_refs):
        dst[...] = src[0].astype(BF16)


def _mm_rope_body(a_ref, w_ref, c_ref, s_ref, o_ref, *, n_rope_tiles):
    rotated = pl.program_id(1) < n_rope_tiles
    bm, bn = o_ref.shape
    quarter = HEAD_DIM // 4
    reps = bn // HEAD_DIM
    for rows in (slice(0, bm // 2), slice(bm // 2, bm)):
        acc = jnp.dot(a_ref[rows, :], w_ref[...], preferred_element_type=F32)
        lane = lax.broadcasted_iota(jnp.int32, acc.shape, 1)
        partner = jnp.where((lane & quarter) == 0,
                            pltpu.roll(acc, bn - quarter, 1), pltpu.roll(acc, quarter, 1))
        cos = jnp.concatenate([jnp.where(rotated, c_ref[rows, :], 1.0)] * reps, axis=1)
        sin = jnp.concatenate([jnp.where(rotated, s_ref[rows, :], 0.0)] * reps, axis=1)
        o_ref[rows, :] = (acc * cos + partner * sin).astype(o_ref.dtype)


def _matmul_rope(a, w, cos, sin, n_rope_cols):
    m, k = a.shape
    n = w.shape[1]
    bm, _ = _mm_tiles(m, n)
    bn = _pick(n - n_rope_cols, (1024, 512, 256, 128))
    assert n_rope_cols % bn == 0
    body = functools.partial(_mm_rope_body, n_rope_tiles=n_rope_cols // bn)
    return pl.pallas_call(
        body, grid=(m // bm, n // bn),
        in_specs=[pl.BlockSpec((bm, k), lambda i, j: (i, 0)), pl.BlockSpec((k, bn), lambda i, j: (0, j)),
                  pl.BlockSpec((bm, HEAD_DIM), lambda i, j: (i, 0)),
                  pl.BlockSpec((bm, HEAD_DIM), lambda i, j: (i, 0))],
        out_specs=pl.BlockSpec((bm, bn), lambda i, j: (i, j)),
        out_shape=jax.ShapeDtypeStruct((m, n), BF16), name="qkv_rope",
        compiler_params=_params("parallel", "parallel"))(a, w, cos, sin)


MIXER_GATE, MLP_GATE = 2, 5


def _row_gate(m_ref, which, bm, n_x_rows):
    row = pl.program_id(0) * bm + lax.broadcasted_iota(jnp.int32, (bm, 1), 0)
    return jnp.where(row >= n_x_rows, m_ref[0, 1, which:which + 1, :], m_ref[0, 0, which:which + 1, :])


def _gated_residual(y, x_ref, m_ref, *, n_x_rows, alpha):
    return alpha * x_ref[...] + _row_gate(m_ref, MIXER_GATE, y.shape[0], n_x_rows) * y


def _mm_resid_body(a_ref, w_ref, x_ref, m_ref, o_ref, *, n_x_rows, alpha):
    y = jnp.dot(a_ref[...], w_ref[...], preferred_element_type=F32)
    o_ref[...] = _gated_residual(y, x_ref, m_ref, n_x_rows=n_x_rows, alpha=alpha)


def _matmul_resid(a, w, x, mods, *, layer, n_x_rows, alpha):
    m, k = a.shape
    n = w.shape[1]
    bm, bn = _mm_tiles(m, n)
    return pl.pallas_call(
        functools.partial(_mm_resid_body, n_x_rows=n_x_rows, alpha=alpha), grid=(m // bm, n // bn),
        in_specs=[pl.BlockSpec((bm, k), lambda i, j: (i, 0)), pl.BlockSpec((k, bn), lambda i, j: (0, j)),
                  pl.BlockSpec((bm, bn), lambda i, j: (i, j)),
                  pl.BlockSpec((1, 2, 6, bn), lambda i, j: (layer, 0, 0, j))],
        out_specs=pl.BlockSpec((bm, bn), lambda i, j: (i, j)),
        out_shape=jax.ShapeDtypeStruct((m, n), F32), name="matmul_resid",
        compiler_params=_params("parallel", "parallel"))(a, w, x, mods)


def _mm_gate_body(a_ref, wb_ref, wc_ref, wx_ref, b_ref, v_ref):
    a = a_ref[...]
    b_ref[...] = jnp.dot(a, wb_ref[...], preferred_element_type=F32).astype(BF16)
    v_ref[...] = (jnp.dot(a, wc_ref[...], preferred_element_type=F32)
                  * jnp.dot(a, wx_ref[...], preferred_element_type=F32)).astype(BF16)


def _matmul_gates(a, w_in):
    m, k = a.shape
    d = w_in.shape[1] // 3
    bm = _pick(m, (768, 640, 512, 256, 128))
    bn = _pick(d, (512, 256, 128))
    nj = d // bn
    a_spec = pl.BlockSpec((bm, k), lambda i, j: (i, 0))
    out = pl.BlockSpec((bm, bn), lambda i, j: (i, j))
    return pl.pallas_call(
        _mm_gate_body, grid=(m // bm, nj),
        in_specs=[a_spec,
                  pl.BlockSpec((k, bn), lambda i, j: (0, j)),
                  pl.BlockSpec((k, bn), lambda i, j: (0, j + nj)),
                  pl.BlockSpec((k, bn), lambda i, j: (0, j + 2 * nj))],
        out_specs=[out, out],
        out_shape=[jax.ShapeDtypeStruct((m, d), BF16), jax.ShapeDtypeStruct((m, d), BF16)], name="conv_in_gates",
        compiler_params=_params("parallel", "parallel"))(a, w_in, w_in, w_in)


def _mlp_body(h_ref, w1_ref, w2_ref, *rest, n_side):
    src_refs, o_ref, dst_refs = rest[:n_side], rest[n_side], rest[n_side + 1:]

    def step(first):
        u = jnp.maximum(jnp.dot(h_ref[...], w1_ref[...], preferred_element_type=F32), 0.0)
        y = jnp.dot((u * u).astype(BF16), w2_ref[...], preferred_element_type=F32)
        if first:
            o_ref[...] = y
        else:
            o_ref[...] += y
        _cast_side(src_refs, dst_refs)

    pl.when(pl.program_id(1) == 0)(functools.partial(step, True))
    pl.when(pl.program_id(1) != 0)(functools.partial(step, False))


def _mlp(h, w1, w2, side=()):
    m, d = h.shape
    f = w1.shape[1]
    bm = _pick(m, ROW_TILES)
    bf = _pick(f, (512, 256, 128))
    n_f = f // bf
    out_mode = {} if 2 * bm * d * 4 <= VMEM_LIMIT // 2 else {"pipeline_mode": pl.Buffered(1)}
    side_in, side_out, side_shape, side_args = _side_casts(
        side, (m // bm) * n_f, lambda i, j: i * n_f + j)
    out = pl.pallas_call(
        functools.partial(_mlp_body, n_side=len(side)), grid=(m // bm, n_f),
        in_specs=[pl.BlockSpec((bm, d), lambda i, j: (i, 0), pipeline_mode=pl.Buffered(1)),
                  pl.BlockSpec((d, bf), lambda i, j: (0, j)),
                  pl.BlockSpec((bf, d), lambda i, j: (j, 0))] + side_in,
        out_specs=[pl.BlockSpec((bm, d), lambda i, j: (i, 0), **out_mode)] + side_out,
        out_shape=[jax.ShapeDtypeStruct((m, d), F32)] + side_shape, name="mlp",
        compiler_params=_params("arbitrary", "arbitrary"))(h, w1, w2, *side_args)
    return out[0], out[1:]


LOG2E = 1.4426950408889634


def _attn_body(sink_ref, bias_ref, q_ref, kp_ref, kc_ref, kn_ref, kx_ref, vp_ref, vc_ref, vn_ref, vx_ref,
               *rest, kv_per_step, n_side):
    src_refs, o_ref, dst_refs = rest[:n_side], rest[n_side], rest[n_side + 1:]
    _attn_heads(sink_ref, bias_ref, q_ref, kp_ref, kc_ref, kn_ref, kx_ref, vp_ref, vc_ref, vn_ref, vx_ref,
                o_ref, kv_per_step=kv_per_step)
    _cast_side(src_refs, dst_refs)


def _attn_heads(sink_ref, bias_ref, q_ref, kp_ref, kc_ref, kn_ref, kx_ref, vp_ref, vc_ref, vn_ref, vx_ref,
                o_ref, *, kv_per_step):
    scale = HEAD_DIM ** -0.5
    bias = bias_ref[0]
    for kv in range(kv_per_step):
        cols = slice(kv * HEAD_DIM, (kv + 1) * HEAD_DIM)
        heads = [(kv * KV_GROUP + g) for g in range(KV_GROUP)]
        q4 = jnp.concatenate([q_ref[:, hd * HEAD_DIM:(hd + 1) * HEAD_DIM] for hd in heads], axis=0)
        kcat = jnp.concatenate([kp_ref[:, cols], kc_ref[:, cols], kn_ref[:, cols], kx_ref[:, cols]], axis=0)
        vcat = jnp.concatenate([vp_ref[:, cols], vc_ref[:, cols], vn_ref[:, cols], vx_ref[:, cols]], axis=0)
        s = lax.dot_general(q4, kcat, (((1,), (1,)), ((), ())), preferred_element_type=F32)
        for g, hd in enumerate(heads):
            sg = s[g * ATT_BLOCK:(g + 1) * ATT_BLOCK] + bias
            sk = sink_ref[pl.program_id(0) * kv_per_step * KV_GROUP + hd] * (1.0 / scale)
            mx = jnp.maximum(jnp.max(sg, axis=1, keepdims=True), sk)
            p = jnp.exp2((sg - mx) * (scale * LOG2E))
            den = jnp.sum(p, axis=1, keepdims=True) + jnp.exp2((sk - mx) * (scale * LOG2E))
            og = jnp.dot(p.astype(BF16), vcat, preferred_element_type=F32)
            o_ref[:, hd * HEAD_DIM:(hd + 1) * HEAD_DIM] = (og / den).astype(o_ref.dtype)


def _band_bias(n_ctx_keys):
    blk = ATT_BLOCK
    nk = 3 * blk + n_ctx_keys
    qi = jnp.arange(blk)[:, None]
    kj = jnp.arange(nk)[None, :]
    rel = kj - blk - qi
    in_window = (jnp.abs(rel) <= blk) & (kj < 3 * blk)
    is_ctx_key = kj >= 3 * blk
    variants = []
    for v in range(5):
        ok = in_window
        if v & 1:
            ok = ok & (kj >= blk)
        if v & 2:
            ok = ok & (kj < 2 * blk)
        if v == 4:
            ok = jnp.zeros_like(in_window)
        variants.append(jnp.where(ok | is_ctx_key, 0.0, -jnp.inf))
    return jnp.stack(variants).astype(F32)


def _attention(qkv, sink, *, n_x_rows, n_heads, q_rows, side=()):
    t = qkv.shape[0]
    c = t - n_x_rows
    blk = ATT_BLOCK
    assert n_x_rows % c == 0 and c % blk == 0
    nkv = n_heads // KV_GROUP
    per = _pick(nkv, (4, 2, 1))
    nbx = n_x_rows // blk
    kcol = n_heads // per
    vcol = (n_heads + nkv) // per
    clampx = lambda b: jnp.clip(b, 0, nbx - 1)

    def band(col, shift):
        return pl.BlockSpec((blk, per * HEAD_DIM), lambda kh, n: (clampx(n + shift), col + kh))

    def ctx(col):
        return pl.BlockSpec((c, per * HEAD_DIM), lambda kh, n: (n_x_rows // c, col + kh))

    def variant(kh, n):
        edge = (n == 0).astype(jnp.int32) + 2 * (n == nbx - 1).astype(jnp.int32)
        return (jnp.where(n >= nbx, 4, edge), 0, 0)

    bias = _band_bias(c)
    qcols = per * KV_GROUP * HEAD_DIM
    n_q = q_rows // blk
    side_in, side_out, side_shape, side_args = _side_casts(
        side, (nkv // per) * n_q, lambda kh, n: kh * n_q + n)
    out = pl.pallas_call(
        functools.partial(_attn_body, kv_per_step=per, n_side=len(side)), grid=(nkv // per, n_q),
        in_specs=[pl.BlockSpec(memory_space=pltpu.SMEM),
                  pl.BlockSpec((1,) + bias.shape[1:], variant),
                  pl.BlockSpec((blk, qcols), lambda kh, n: (n, kh)),
                  band(kcol, -1), band(kcol, 0), band(kcol, 1), ctx(kcol),
                  band(vcol, -1), band(vcol, 0), band(vcol, 1), ctx(vcol)] + side_in,
        out_specs=[pl.BlockSpec((blk, qcols), lambda kh, n: (n, kh))] + side_out,
        out_shape=[jax.ShapeDtypeStruct((q_rows, n_heads * HEAD_DIM), BF16)] + side_shape, name="attention",
        compiler_params=_params("arbitrary", "arbitrary"))(sink, bias, qkv, *([qkv] * 8), *side_args)
    return out[0], out[1:]


def _rope_tables(n_x_rows, n_rows):
    tpos = jnp.arange(n_x_rows)
    n_freq = HEAD_DIM // 4
    freqs = ROPE_THETA ** (-jnp.arange(n_freq, dtype=F32) / n_freq)
    ang_r = (tpos // GRID_W).astype(F32)[:, None] * freqs
    ang_c = (tpos % GRID_W).astype(F32)[:, None] * freqs
    cos = jnp.concatenate([jnp.cos(ang_r)] * 2 + [jnp.cos(ang_c)] * 2, axis=1)
    sin = jnp.concatenate([-jnp.sin(ang_r), jnp.sin(ang_r), -jnp.sin(ang_c), jnp.sin(ang_c)], axis=1)
    pad = n_rows - n_x_rows
    cos = jnp.concatenate([cos, jnp.ones((pad, HEAD_DIM), F32)], axis=0)
    sin = jnp.concatenate([sin, jnp.zeros((pad, HEAD_DIM), F32)], axis=0)
    return cos, sin


def kernel(x, c, ctx, c_ctx, mod_w_down, mod_w_up, mod_b, ln_g, ln_b, mlp_w1, mlp_w2,
           attn_w_qkv, attn_w_o, attn_sink, pool_w, pool_scale,
           conv_w_in, conv_w, conv_b, conv_w_out):
    batch, s, d = x.shape
    assert batch == 1 and s % ROW_TILE == 0 and ctx.shape[1] % ROW_TILE == 0
    depth = mod_w_down.shape[0]
    t = s + ctx.shape[1]
    n_heads = d // HEAD_DIM
    alpha = (2 * depth) ** 0.25

    cond = jnp.zeros((8, d), F32).at[0].set(c[0]).at[1].set(c_ctx)
    cond = jnp.broadcast_to(cond, (depth, 8, d))
    mods = _vecmat(_vecmat(cond, mod_w_down, silu=True), mod_w_up, mod_b)
    mods = mods[:, :2].reshape(depth, 2, 6, d)

    cos, sin = _rope_tables(s, t)
    xs, h = _prep(x, ctx, mods)

    pool_w2d = pool_w.reshape(pool_w.shape[0], -1, pool_w.shape[-1])

    def layer_weights(i):
        kind, j = i % N_MIXERS, i // N_MIXERS
        mixer = {0: [("qkv", attn_w_qkv, j), ("wo", attn_w_o, j)],
                 1: [("pool", pool_w2d, j)],
                 2: [("cin", conv_w_in, j), ("cout", conv_w_out, j)]}[kind]
        return mixer + [("w1", mlp_w1, i), ("w2", mlp_w2, i)]

    wb = {name: w[idx].astype(BF16) for name, w, idx in layer_weights(0)[:-2]}

    for i in range(depth):
        last = i == depth - 1
        rows = s if last else t
        kind, j = i % N_MIXERS, i // N_MIXERS
        if kind == 0:
            n_rope = (n_heads + n_heads // KV_GROUP) * HEAD_DIM
            qkv = _matmul_rope(h, wb["qkv"], cos, sin, n_rope)
            own_mlp = [] if "w1" in wb else layer_weights(i)[-2:]
            o, cast = _attention(qkv, attn_sink[j], n_x_rows=s, n_heads=n_heads, q_rows=rows,
                                 side=[(w, idx) for _, w, idx in own_mlp])
            wb.update({name: wc for (name, _, _), wc in zip(own_mlp, cast)})
            z = _matmul_resid(o, wb["wo"], xs, mods, layer=i, n_x_rows=s, alpha=alpha)
        elif kind == 1:
            z = _pool_mixer(xs, wb["pool"].reshape(pool_w.shape[1:]), pool_scale[j], mods,
                            layer=i, n_x_rows=s, alpha=alpha)
        else:
            b, v = _matmul_gates(h, wb["cin"])
            g = _gated_conv(b, v, conv_w, conv_b, layer=j, n_x_rows=s)
            z = _matmul_resid(g, wb["cout"], xs, mods, layer=i, n_x_rows=s, alpha=alpha)
        xs, h = _ln(z, mods, ln_g, ln_b, layer=i, which=0, next_layer=i, n_x_rows=s, rows=rows, emit_h=True)
        for name, w, idx in layer_weights(i)[-2:]:
            if name not in wb:
                wb[name] = w[idx].astype(BF16)
        nxt = [] if last else layer_weights(i + 1)
        y, cast = _mlp(h, wb["w1"], wb["w2"], [(w, idx) for _, w, idx in nxt])
        wb = {name: wc for (name, _, _), wc in zip(nxt, cast)}
        need_h = (not last) and (i + 1) % N_MIXERS != 1
        xs, h = _ln(y, mods, ln_g, ln_b, layer=i, which=1, next_layer=min(i + 1, depth - 1),
                    n_x_rows=s, rows=rows, emit_h=need_h, resid=(xs, alpha))
    return xs.reshape(batch, s, d)
```

```python
import functools

import jax
import jax.numpy as jnp
from jax import lax
from jax.experimental import pallas as pl
from jax.experimental.pallas import tpu as pltpu

HEAD_DIM = 128
KV_GROUP = 4
GRID_W = 64
ATT_BLOCK = 128
ROPE_THETA = 10000.0
POOL_WINDOWS = (2, 4, 8, 16)
CONV_WIDTH = 3
N_MIXERS = 3
LN_EPS = 1e-5

LANES = 128
ROW_TILE = 256
HALO = 16
VMEM_LIMIT = 58 * 1024 * 1024

F32 = jnp.float32
BF16 = jnp.bfloat16


def _pick(n, candidates):
    for c in candidates:
        if n % c == 0:
            return c
    raise ValueError(f"no tile in {candidates} divides {n}")


def _params(*sem):
    return pltpu.CompilerParams(dimension_semantics=sem, vmem_limit_bytes=VMEM_LIMIT)


def _vecmat_body(a_ref, w_ref, o_ref, *, silu):
    a = a_ref[0]
    if silu:
        a = a * jax.nn.sigmoid(a)
    o_ref[0] = jnp.dot(a.astype(BF16), w_ref[0].astype(BF16), preferred_element_type=F32)


def _vecmat_bias_body(a_ref, w_ref, b_ref, o_ref):
    o_ref[0] = jnp.dot(a_ref[0].astype(BF16), w_ref[0].astype(BF16),
                       preferred_element_type=F32) + b_ref[0]


def _vecmat(a, w, bias=None, *, silu=False):
    nl, rows, k = a.shape
    n = w.shape[2]
    bn = _pick(n, (2048, 1024, 512, 256, 128))
    in_specs = [pl.BlockSpec((1, rows, k), lambda l, j: (l, 0, 0)),
                pl.BlockSpec((1, k, bn), lambda l, j: (l, 0, j))]
    args = [a, w]
    if bias is None:
        body = functools.partial(_vecmat_body, silu=silu)
    else:
        body = _vecmat_bias_body
        in_specs.append(pl.BlockSpec((1, 1, bn), lambda l, j: (l, 0, j)))
        args.append(bias.reshape(nl, 1, n))
    return pl.pallas_call(
        body, grid=(nl, n // bn), in_specs=in_specs,
        out_specs=pl.BlockSpec((1, rows, bn), lambda l, j: (l, 0, j)),
        out_shape=jax.ShapeDtypeStruct((nl, rows, n), F32), name="adaln_vecmat",
        compiler_params=_params("parallel", "parallel"))(*args)


def _prep_body(x_ref, c_ref, m_ref, xo_ref, ho_ref, *, n_x_tiles):
    is_ctx = pl.program_id(0) >= n_x_tiles
    xv = jnp.where(is_ctx, c_ref[0], x_ref[0])
    xo_ref[...] = xv
    ho_ref[...] = (xv * (1.0 + m_ref[0, 0, 1:2, :]) + m_ref[0, 0, 0:1, :]).astype(BF16)


def _prep(x, ctx, mods):
    _, s, d = x.shape
    c = ctx.shape[1]
    t = s + c
    rt = ROW_TILE
    nx = s // rt
    return pl.pallas_call(
        functools.partial(_prep_body, n_x_tiles=nx), grid=(t // rt,),
        in_specs=[pl.BlockSpec((1, rt, d), lambda i: (0, jnp.minimum(i, nx - 1), 0)),
                  pl.BlockSpec((1, rt, d), lambda i: (0, jnp.maximum(i - nx, 0), 0)),
                  pl.BlockSpec((1, 1, 6, d), lambda i: (0, (i >= nx).astype(jnp.int32), 0, 0))],
        out_specs=[pl.BlockSpec((rt, d), lambda i: (i, 0)), pl.BlockSpec((rt, d), lambda i: (i, 0))],
        out_shape=[jax.ShapeDtypeStruct((t, d), F32), jax.ShapeDtypeStruct((t, d), BF16)], name="prep",
        compiler_params=_params("parallel"))(x, ctx, mods)


def _ln_body(*refs, shift, scale, which, alpha):
    if alpha is None:
        z_ref, mn_ref, g_ref, b_ref, xo_ref, *ho_ref = refs
        z = z_ref[...]
    else:
        x_ref, y_ref, mg_ref, mn_ref, g_ref, b_ref, xo_ref, *ho_ref = refs
        gate = MIXER_GATE if which == 0 else MLP_GATE
        z = alpha * x_ref[...] + mg_ref[0, 0, gate:gate + 1, :] * y_ref[...]
    mu = jnp.mean(z, axis=-1, keepdims=True)
    zc = z - mu
    var = jnp.mean(zc * zc, axis=-1, keepdims=True)
    xn = zc * lax.rsqrt(var + LN_EPS) * g_ref[0, which:which + 1, :] + b_ref[0, which:which + 1, :]
    xo_ref[...] = xn
    if ho_ref:
        ho_ref[0][...] = (xn * (1.0 + mn_ref[0, 0, scale:scale + 1, :])
                          + mn_ref[0, 0, shift:shift + 1, :]).astype(BF16)


def _ln(z, mods, ln_g, ln_b, *, layer, which, next_layer, n_x_rows, rows, emit_h, resid=None):
    d = z.shape[1]
    rt = ROW_TILE
    nx = n_x_rows // rt
    shift, scale = (3, 4) if which == 0 else (0, 1)
    is_ctx = lambda i: (i >= nx).astype(jnp.int32)
    row = pl.BlockSpec((rt, d), lambda i: (i, 0))
    out_specs = [row]
    out_shape = [jax.ShapeDtypeStruct((rows, d), F32)]
    if emit_h:
        out_specs.append(row)
        out_shape.append(jax.ShapeDtypeStruct((rows, d), BF16))
    ln_spec = pl.BlockSpec((1, 2, d), lambda i: (layer, 0, 0))
    mod_spec = lambda l: pl.BlockSpec((1, 1, 6, d), lambda i: (l, is_ctx(i), 0, 0))
    if resid is None:
        alpha, in_specs, args = None, [row, mod_spec(next_layer), ln_spec, ln_spec], (z, mods, ln_g, ln_b)
    else:
        x, alpha = resid
        in_specs = [row, row, mod_spec(layer), mod_spec(next_layer), ln_spec, ln_spec]
        args = (x, z, mods, mods, ln_g, ln_b)
    out = pl.pallas_call(
        functools.partial(_ln_body, shift=shift, scale=scale, which=which, alpha=alpha),
        grid=(rows // rt,), in_specs=in_specs,
        out_specs=out_specs, out_shape=out_shape, name="layernorm",
        compiler_params=_params("parallel"))(*args)
    return (out[0], out[1]) if emit_h else (out[0], None)


def _seq_bounds(i, rt, n_x_rows, n_rows):
    in_x = i * rt < n_x_rows
    return jnp.where(in_x, 0, n_x_rows), jnp.where(in_x, n_x_rows, n_rows)


def _pool_body(xp_ref, x_ref, xn_ref, m_ref, w_ref, s_ref, o_ref, *, n_x_rows, n_rows, alpha):
    rt = x_ref.shape[0]
    n_groups, group, _ = w_ref.shape
    i = pl.program_id(0)
    lo, hi = _seq_bounds(i, rt, n_x_rows, n_rows)
    r = i * rt + lax.broadcasted_iota(jnp.int32, (rt, 1), 0)
    xx = jnp.concatenate([xp_ref[...], x_ref[...], xn_ref[...]], axis=0)
    hh = xx * (1.0 + m_ref[0, 0, 1:2, :]) + m_ref[0, 0, 0:1, :]
    rr = i * rt - HALO + lax.broadcasted_iota(jnp.int32, (rt + 2 * HALO, 1), 0)
    hm = jnp.where((rr >= lo) & (rr < hi), hh, 0.0)
    for g, w in enumerate(POOL_WINDOWS):
        assert w & (w - 1) == 0 and w // 2 <= HALO
        cols = slice(g * group, (g + 1) * group)
        run = hm[:, cols]
        width = 1
        while width < w:
            run = run[:run.shape[0] - width] + run[width:]
            width *= 2
        first = HALO - w // 2
        cnt = (jnp.minimum(r + (w - w // 2), hi) - jnp.maximum(r - w // 2, lo)).astype(F32)
        diff = (run[first:first + rt] * (1.0 / cnt) - hh[HALO:HALO + rt, cols]).astype(BF16)
        y = jnp.dot(diff, w_ref[g], preferred_element_type=F32) * s_ref[:, cols]
        o_ref[:, cols] = alpha * x_ref[:, cols] + m_ref[0, 0, MIXER_GATE:MIXER_GATE + 1, cols] * y


def _halo_specs(rt, d, n_rows):
    per = rt // HALO
    last = n_rows // HALO - 1
    return (pl.BlockSpec((HALO, d), lambda i: (jnp.maximum(i * per - 1, 0), 0)),
            pl.BlockSpec((rt, d), lambda i: (i, 0)),
            pl.BlockSpec((HALO, d), lambda i: (jnp.minimum((i + 1) * per, last), 0)))


def _pool_mixer(x, w, ch_scale, mods, *, layer, n_x_rows, alpha):
    t, d = x.shape
    rt = ROW_TILE
    nx = n_x_rows // rt
    prev, cur, nxt = _halo_specs(rt, d, t)
    body = functools.partial(_pool_body, n_x_rows=n_x_rows, n_rows=t, alpha=alpha)
    return pl.pallas_call(
        body, grid=(t // rt,),
        in_specs=[prev, cur, nxt,
                  pl.BlockSpec((1, 1, 6, d), lambda i: (layer, (i >= nx).astype(jnp.int32), 0, 0)),
                  pl.BlockSpec(w.shape, lambda i: (0, 0, 0), pipeline_mode=pl.Buffered(1)),
                  pl.BlockSpec((1, d), lambda i: (0, 0))],
        out_specs=pl.BlockSpec((rt, d), lambda i: (i, 0)),
        out_shape=jax.ShapeDtypeStruct((t, d), F32), name="pool_mixer",
        compiler_params=_params("parallel"))(x, x, x, mods, w, ch_scale.reshape(1, -1))


def _conv_body(b_ref, vp_ref, v_ref, vn_ref, w_ref, cb_ref, o_ref, *, n_x_rows, n_rows):
    rt = v_ref.shape[0]
    i = pl.program_id(0)
    lo, hi = _seq_bounds(i, rt, n_x_rows, n_rows)
    r = i * rt + lax.broadcasted_iota(jnp.int32, (rt, 1), 0)
    vv = jnp.concatenate([vp_ref[...], v_ref[...], vn_ref[...]], axis=0).astype(F32)
    pad = (CONV_WIDTH - 1) // 2
    acc = jnp.zeros(v_ref.shape, F32)
    for k in range(CONV_WIDTH):
        off = k - pad
        ok = (r + off >= lo) & (r + off < hi)
        acc = acc + jnp.where(ok, vv[HALO + off:HALO + off + rt], 0.0) * w_ref[0, k:k + 1, :]
    o_ref[...] = (b_ref[...].astype(F32) * (acc + cb_ref[0])).astype(BF16)


def _gated_conv(b, v, conv_w, conv_b, *, layer, n_x_rows):
    t, d = v.shape
    rt = ROW_TILE
    prev, cur, nxt = _halo_specs(rt, d, t)
    body = functools.partial(_conv_body, n_x_rows=n_x_rows, n_rows=t)
    return pl.pallas_call(
        body, grid=(t // rt,),
        in_specs=[cur, prev, cur, nxt,
                  pl.BlockSpec((1, CONV_WIDTH, d), lambda i: (layer, 0, 0)),
                  pl.BlockSpec((1, 1, d), lambda i: (layer, 0, 0))],
        out_specs=pl.BlockSpec((rt, d), lambda i: (i, 0)),
        out_shape=jax.ShapeDtypeStruct((t, d), BF16), name="gated_conv",
        compiler_params=_params("parallel"))(b, v, v, v, conv_w, conv_b.reshape(-1, 1, d))


ROW_TILES = (1024, 768, 640, 512, 256, 128)


def _mm_tiles(m, n):
    return _pick(m, ROW_TILES), _pick(n, (1024, 512, 256, 128))


def _side_chunks(rows, n_steps):
    for chunks in range(min(n_steps, rows // HALO), 0, -1):
        if rows % (chunks * HALO) == 0:
            return chunks
    raise ValueError(f"cannot chunk {rows} rows")


def _side_casts(side, n_steps, step_of):
    in_specs, out_specs, out_shape = [], [], []
    for w, layer in side:
        _, rows, cols = w.shape
        chunks = _side_chunks(rows, n_steps)
        chunk = lambda *ids, chunks=chunks: jnp.minimum(step_of(*ids), chunks - 1)
        in_specs.append(pl.BlockSpec((1, rows // chunks, cols),
                                     lambda *ids, layer=layer, chunk=chunk: (layer, chunk(*ids), 0)))
        out_specs.append(pl.BlockSpec((rows // chunks, cols), lambda *ids, chunk=chunk: (chunk(*ids), 0)))
        out_shape.append(jax.ShapeDtypeStruct((rows, cols), BF16))
    return in_specs, out_specs, out_shape, [w for w, _ in side]


def _cast_side(src_refs, dst_refs):
    for src, dst in zip(src_refs, dst_refs):
        dst[...] = src[0].astype(BF16)


def _mm_rope_body(a_ref, w_ref, c_ref, s_ref, o_ref, *, n_rope_tiles):
    rotated = pl.program_id(1) < n_rope_tiles
    bm, bn = o_ref.shape
    quarter = HEAD_DIM // 4
    reps = bn // HEAD_DIM
    for rows in (slice(0, bm // 2), slice(bm // 2, bm)):
        acc = jnp.dot(a_ref[rows, :], w_ref[...], preferred_element_type=F32)
        lane = lax.broadcasted_iota(jnp.int32, acc.shape, 1)
        partner = jnp.where((lane & quarter) == 0,
                            pltpu.roll(acc, bn - quarter, 1), pltpu.roll(acc, quarter, 1))
        cos = jnp.concatenate([jnp.where(rotated, c_ref[rows, :], 1.0)] * reps, axis=1)
        sin = jnp.concatenate([jnp.where(rotated, s_ref[rows, :], 0.0)] * reps, axis=1)
        o_ref[rows, :] = (acc * cos + partner * sin).astype(o_ref.dtype)


def _matmul_rope(a, w, cos, sin, n_rope_cols):
    m, k = a.shape
    n = w.shape[1]
    bm, _ = _mm_tiles(m, n)
    bn = _pick(n - n_rope_cols, (1024, 512, 256, 128))
    assert n_rope_cols % bn == 0
    body = functools.partial(_mm_rope_body, n_rope_tiles=n_rope_cols // bn)
    return pl.pallas_call(
        body, grid=(m // bm, n // bn),
        in_specs=[pl.BlockSpec((bm, k), lambda i, j: (i, 0)), pl.BlockSpec((k, bn), lambda i, j: (0, j)),
                  pl.BlockSpec((bm, HEAD_DIM), lambda i, j: (i, 0)),
                  pl.BlockSpec((bm, HEAD_DIM), lambda i, j: (i, 0))],
        out_specs=pl.BlockSpec((bm, bn), lambda i, j: (i, j)),
        out_shape=jax.ShapeDtypeStruct((m, n), BF16), name="qkv_rope",
        compiler_params=_params("parallel", "parallel"))(a, w, cos, sin)


MIXER_GATE, MLP_GATE = 2, 5


def _row_gate(m_ref, which, bm, n_x_rows):
    row = pl.program_id(0) * bm + lax.broadcasted_iota(jnp.int32, (bm, 1), 0)
    return jnp.where(row >= n_x_rows, m_ref[0, 1, which:which + 1, :], m_ref[0, 0, which:which + 1, :])


def _gated_residual(y, x_ref, m_ref, *, n_x_rows, alpha):
    return alpha * x_ref[...] + _row_gate(m_ref, MIXER_GATE, y.shape[0], n_x_rows) * y


def _mm_resid_body(a_ref, w_ref, x_ref, m_ref, o_ref, *, n_x_rows, alpha):
    y = jnp.dot(a_ref[...], w_ref[...], preferred_element_type=F32)
    o_ref[...] = _gated_residual(y, x_ref, m_ref, n_x_rows=n_x_rows, alpha=alpha)


def _matmul_resid(a, w, x, mods, *, layer, n_x_rows, alpha):
    m, k = a.shape
    n = w.shape[1]
    bm, bn = _mm_tiles(m, n)
    return pl.pallas_call(
        functools.partial(_mm_resid_body, n_x_rows=n_x_rows, alpha=alpha), grid=(m // bm, n // bn),
        in_specs=[pl.BlockSpec((bm, k), lambda i, j: (i, 0)), pl.BlockSpec((k, bn), lambda i, j: (0, j)),
                  pl.BlockSpec((bm, bn), lambda i, j: (i, j)),
                  pl.BlockSpec((1, 2, 6, bn), lambda i, j: (layer, 0, 0, j))],
        out_specs=pl.BlockSpec((bm, bn), lambda i, j: (i, j)),
        out_shape=jax.ShapeDtypeStruct((m, n), F32), name="matmul_resid",
        compiler_params=_params("parallel", "parallel"))(a, w, x, mods)


def _mm_gate_body(a_ref, wb_ref, wc_ref, wx_ref, b_ref, v_ref):
    a = a_ref[...]
    b_ref[...] = jnp.dot(a, wb_ref[...], preferred_element_type=F32).astype(BF16)
    v_ref[...] = (jnp.dot(a, wc_ref[...], preferred_element_type=F32)
                  * jnp.dot(a, wx_ref[...], preferred_element_type=F32)).astype(BF16)


def _matmul_gates(a, w_in):
    m, k = a.shape
    d = w_in.shape[1] // 3
    bm = _pick(m, (768, 640, 512, 256, 128))
    bn = _pick(d, (512, 256, 128))
    nj = d // bn
    a_spec = pl.BlockSpec((bm, k), lambda i, j: (i, 0))
    out = pl.BlockSpec((bm, bn), lambda i, j: (i, j))
    return pl.pallas_call(
        _mm_gate_body, grid=(m // bm, nj),
        in_specs=[a_spec,
                  pl.BlockSpec((k, bn), lambda i, j: (0, j)),
                  pl.BlockSpec((k, bn), lambda i, j: (0, j + nj)),
                  pl.BlockSpec((k, bn), lambda i, j: (0, j + 2 * nj))],
        out_specs=[out, out],
        out_shape=[jax.ShapeDtypeStruct((m, d), BF16), jax.ShapeDtypeStruct((m, d), BF16)], name="conv_in_gates",
        compiler_params=_params("parallel", "parallel"))(a, w_in, w_in, w_in)


def _mlp_body(h_ref, w1_ref, w2_ref, *rest, n_side):
    src_refs, o_ref, dst_refs = rest[:n_side], rest[n_side], rest[n_side + 1:]

    def step(first):
        u = jnp.maximum(jnp.dot(h_ref[...], w1_ref[...], preferred_element_type=F32), 0.0)
        y = jnp.dot((u * u).astype(BF16), w2_ref[...], preferred_element_type=F32)
        if first:
            o_ref[...] = y
        else:
            o_ref[...] += y
        _cast_side(src_refs, dst_refs)

    pl.when(pl.program_id(1) == 0)(functools.partial(step, True))
    pl.when(pl.program_id(1) != 0)(functools.partial(step, False))


def _mlp(h, w1, w2, side=()):
    m, d = h.shape
    f = w1.shape[1]
    bm = _pick(m, ROW_TILES)
    bf = _pick(f, (512, 256, 128))
    n_f = f // bf
    out_mode = {} if 2 * bm * d * 4 <= VMEM_LIMIT // 2 else {"pipeline_mode": pl.Buffered(1)}
    side_in, side_out, side_shape, side_args = _side_casts(
        side, (m // bm) * n_f, lambda i, j: i * n_f + j)
    out = pl.pallas_call(
        functools.partial(_mlp_body, n_side=len(side)), grid=(m // bm, n_f),
        in_specs=[pl.BlockSpec((bm, d), lambda i, j: (i, 0), pipeline_mode=pl.Buffered(1)),
                  pl.BlockSpec((d, bf), lambda i, j: (0, j)),
                  pl.BlockSpec((bf, d), lambda i, j: (j, 0))] + side_in,
        out_specs=[pl.BlockSpec((bm, d), lambda i, j: (i, 0), **out_mode)] + side_out,
        out_shape=[jax.ShapeDtypeStruct((m, d), F32)] + side_shape, name="mlp",
        compiler_params=_params("arbitrary", "arbitrary"))(h, w1, w2, *side_args)
    return out[0], out[1:]


LOG2E = 1.4426950408889634


def _attn_body(sink_ref, bias_ref, q_ref, kp_ref, kc_ref, kn_ref, kx_ref, vp_ref, vc_ref, vn_ref, vx_ref,
               *rest, kv_per_step, n_side):
    src_refs, o_ref, dst_refs = rest[:n_side], rest[n_side], rest[n_side + 1:]
    _attn_heads(sink_ref, bias_ref, q_ref, kp_ref, kc_ref, kn_ref, kx_ref, vp_ref, vc_ref, vn_ref, vx_ref,
                o_ref, kv_per_step=kv_per_step)
    _cast_side(src_refs, dst_refs)


def _attn_heads(sink_ref, bias_ref, q_ref, kp_ref, kc_ref, kn_ref, kx_ref, vp_ref, vc_ref, vn_ref, vx_ref,
                o_ref, *, kv_per_step):
    scale = HEAD_DIM ** -0.5
    bias = bias_ref[0]
    for kv in range(kv_per_step):
        cols = slice(kv * HEAD_DIM, (kv + 1) * HEAD_DIM)
        heads = [(kv * KV_GROUP + g) for g in range(KV_GROUP)]
        q4 = jnp.concatenate([q_ref[:, hd * HEAD_DIM:(hd + 1) * HEAD_DIM] for hd in heads], axis=0)
        kcat = jnp.concatenate([kp_ref[:, cols], kc_ref[:, cols], kn_ref[:, cols], kx_ref[:, cols]], axis=0)
        vcat = jnp.concatenate([vp_ref[:, cols], vc_ref[:, cols], vn_ref[:, cols], vx_ref[:, cols]], axis=0)
        s = lax.dot_general(q4, kcat, (((1,), (1,)), ((), ())), preferred_element_type=F32)
        for g, hd in enumerate(heads):
            sg = s[g * ATT_BLOCK:(g + 1) * ATT_BLOCK] + bias
            sk = sink_ref[pl.program_id(0) * kv_per_step * KV_GROUP + hd] * (1.0 / scale)
            mx = jnp.maximum(jnp.max(sg, axis=1, keepdims=True), sk)
            p = jnp.exp2((sg - mx) * (scale * LOG2E))
            den = jnp.sum(p, axis=1, keepdims=True) + jnp.exp2((sk - mx) * (scale * LOG2E))
            og = jnp.dot(p.astype(BF16), vcat, preferred_element_type=F32)
            o_ref[:, hd * HEAD_DIM:(hd + 1) * HEAD_DIM] = (og / den).astype(o_ref.dtype)


def _band_bias(n_ctx_keys):
    blk = ATT_BLOCK
    nk = 3 * blk + n_ctx_keys
    qi = jnp.arange(blk)[:, None]
    kj = jnp.arange(nk)[None, :]
    rel = kj - blk - qi
    in_window = (jnp.abs(rel) <= blk) & (kj < 3 * blk)
    is_ctx_key = kj >= 3 * blk
    variants = []
    for v in range(5):
        ok = in_window
        if v & 1:
            ok = ok & (kj >= blk)
        if v & 2:
            ok = ok & (kj < 2 * blk)
        if v == 4:
            ok = jnp.zeros_like(in_window)
        variants.append(jnp.where(ok | is_ctx_key, 0.0, -jnp.inf))
    return jnp.stack(variants).astype(F32)


def _attention(qkv, sink, *, n_x_rows, n_heads, q_rows, side=()):
    t = qkv.shape[0]
    c = t - n_x_rows
    blk = ATT_BLOCK
    assert n_x_rows % c == 0 and c % blk == 0
    nkv = n_heads // KV_GROUP
    per = _pick(nkv, (8, 4, 2, 1))
    nbx = n_x_rows // blk
    kcol = n_heads // per
    vcol = (n_heads + nkv) // per
    clampx = lambda b: jnp.clip(b, 0, nbx - 1)

    def band(col, shift):
        return pl.BlockSpec((blk, per * HEAD_DIM), lambda kh, n: (clampx(n + shift), col + kh))

    def ctx(col):
        return pl.BlockSpec((c, per * HEAD_DIM), lambda kh, n: (n_x_rows // c, col + kh))

    def variant(kh, n):
        edge = (n == 0).astype(jnp.int32) + 2 * (n == nbx - 1).astype(jnp.int32)
        return (jnp.where(n >= nbx, 4, edge), 0, 0)

    bias = _band_bias(c)
    qcols = per * KV_GROUP * HEAD_DIM
    n_q = q_rows // blk
    side_in, side_out, side_shape, side_args = _side_casts(
        side, (nkv // per) * n_q, lambda kh, n: kh * n_q + n)
    out = pl.pallas_call(
        functools.partial(_attn_body, kv_per_step=per, n_side=len(side)), grid=(nkv // per, n_q),
        in_specs=[pl.BlockSpec(memory_space=pltpu.SMEM),
                  pl.BlockSpec((1,) + bias.shape[1:], variant),
                  pl.BlockSpec((blk, qcols), lambda kh, n: (n, kh)),
                  band(kcol, -1), band(kcol, 0), band(kcol, 1), ctx(kcol),
                  band(vcol, -1), band(vcol, 0), band(vcol, 1), ctx(vcol)] + side_in,
        out_specs=[pl.BlockSpec((blk, qcols), lambda kh, n: (n, kh))] + side_out,
        out_shape=[jax.ShapeDtypeStruct((q_rows, n_heads * HEAD_DIM), BF16)] + side_shape, name="attention",
        compiler_params=_params("arbitrary", "arbitrary"))(sink, bias, qkv, *([qkv] * 8), *side_args)
    return out[0], out[1:]


def _rope_tables(n_x_rows, n_rows):
    tpos = jnp.arange(n_x_rows)
    n_freq = HEAD_DIM // 4
    freqs = ROPE_THETA ** (-jnp.arange(n_freq, dtype=F32) / n_freq)
    ang_r = (tpos // GRID_W).astype(F32)[:, None] * freqs
    ang_c = (tpos % GRID_W).astype(F32)[:, None] * freqs
    cos = jnp.concatenate([jnp.cos(ang_r)] * 2 + [jnp.cos(ang_c)] * 2, axis=1)
    sin = jnp.concatenate([-jnp.sin(ang_r), jnp.sin(ang_r), -jnp.sin(ang_c), jnp.sin(ang_c)], axis=1)
    pad = n_rows - n_x_rows
    cos = jnp.concatenate([cos, jnp.ones((pad, HEAD_DIM), F32)], axis=0)
    sin = jnp.concatenate([sin, jnp.zeros((pad, HEAD_DIM), F32)], axis=0)
    return cos, sin


def kernel(x, c, ctx, c_ctx, mod_w_down, mod_w_up, mod_b, ln_g, ln_b, mlp_w1, mlp_w2,
           attn_w_qkv, attn_w_o, attn_sink, pool_w, pool_scale,
           conv_w_in, conv_w, conv_b, conv_w_out):
    batch, s, d = x.shape
    assert batch == 1 and s % ROW_TILE == 0 and ctx.shape[1] % ROW_TILE == 0
    depth = mod_w_down.shape[0]
    t = s + ctx.shape[1]
    n_heads = d // HEAD_DIM
    alpha = (2 * depth) ** 0.25

    cond = jnp.zeros((8, d), F32).at[0].set(c[0]).at[1].set(c_ctx)
    cond = jnp.broadcast_to(cond, (depth, 8, d))
    mods = _vecmat(_vecmat(cond, mod_w_down, silu=True), mod_w_up, mod_b)
    mods = mods[:, :2].reshape(depth, 2, 6, d)

    cos, sin = _rope_tables(s, t)
    xs, h = _prep(x, ctx, mods)

    pool_w2d = pool_w.reshape(pool_w.shape[0], -1, pool_w.shape[-1])

    def layer_weights(i):
        kind, j = i % N_MIXERS, i // N_MIXERS
        mixer = {0: [("qkv", attn_w_qkv, j), ("wo", attn_w_o, j)],
                 1: [("pool", pool_w2d, j)],
                 2: [("cin", conv_w_in, j), ("cout", conv_w_out, j)]}[kind]
        return mixer + [("w1", mlp_w1, i), ("w2", mlp_w2, i)]

    wb = {name: w[idx].astype(BF16) for name, w, idx in layer_weights(0)[:-2]}

    for i in range(depth):
        last = i == depth - 1
        rows = s if last else t
        kind, j = i % N_MIXERS, i // N_MIXERS
        if kind == 0:
            n_rope = (n_heads + n_heads // KV_GROUP) * HEAD_DIM
            qkv = _matmul_rope(h, wb["qkv"], cos, sin, n_rope)
            own_mlp = [] if "w1" in wb else layer_weights(i)[-2:]
            o, cast = _attention(qkv, attn_sink[j], n_x_rows=s, n_heads=n_heads, q_rows=rows,
                                 side=[(w, idx) for _, w, idx in own_mlp])
            wb.update({name: wc for (name, _, _), wc in zip(own_mlp, cast)})
            z = _matmul_resid(o, wb["wo"], xs, mods, layer=i, n_x_rows=s, alpha=alpha)
        elif kind == 1:
            z = _pool_mixer(xs, wb["pool"].reshape(pool_w.shape[1:]), pool_scale[j], mods,
                            layer=i, n_x_rows=s, alpha=alpha)
        else:
            b, v = _matmul_gates(h, wb["cin"])
            g = _gated_conv(b, v, conv_w, conv_b, layer=j, n_x_rows=s)
            z = _matmul_resid(g, wb["cout"], xs, mods, layer=i, n_x_rows=s, alpha=alpha)
        xs, h = _ln(z, mods, ln_g, ln_b, layer=i, which=0, next_layer=i, n_x_rows=s, rows=rows, emit_h=True)
        for name, w, idx in layer_weights(i)[-2:]:
            if name not in wb:
                wb[name] = w[idx].astype(BF16)
        nxt = [] if last else layer_weights(i + 1)
        y, cast = _mlp(h, wb["w1"], wb["w2"], [(w, idx) for _, w, idx in nxt])
        wb = {name: wc for (name, _, _), wc in zip(nxt, cast)}
        need_h = (not last) and (i + 1) % N_MIXERS != 1
        xs, h = _ln(y, mods, ln_g, ln_b, layer=i, which=1, next_layer=min(i + 1, depth - 1),
                    n_x_rows=s, rows=rows, emit_h=need_h, resid=(xs, alpha))
    return xs.reshape(batch, s, d)
```
